```python
import math
import jax, jax.numpy as jnp
from jax import lax
import numpy as np

D_MODEL = 1024
BATCH = 4
SEQ = 4096
DEPTH = 4

HEAD_DIM = 64
MIX_WIDTH = D_MODEL
GROUP_WIDTH = MIX_WIDTH // 4
CONV_WIDTH = 3
POOL_WINDOWS = (2, 4, 8, 16)
POOL_GROUP = GROUP_WIDTH // len(POOL_WINDOWS)
DSA_HEADS = GROUP_WIDTH // HEAD_DIM
IDX_HEADS = 8
IDX_DIM = 64
DSA_TOPK_MAX = 256
MOBA_HEADS = GROUP_WIDTH // HEAD_DIM
MOBA_BLOCK = 256
MOBA_TOPB_MAX = 3
REL_BUCKETS = 32
REL_MAX_DIST = 128
N_ATTN_HEADS = DSA_HEADS + MOBA_HEADS
D_FF = -(-8 * D_MODEL // (3 * 256)) * 256
PLE_DIM = 256
Q_BLOCK = 128
MOBA_Q_BLOCK = 64
RMS_EPS = 1e-6
SPLIT_SIZES = (GROUP_WIDTH, GROUP_WIDTH, GROUP_WIDTH,
               GROUP_WIDTH,
               GROUP_WIDTH, GROUP_WIDTH, GROUP_WIDTH, IDX_HEADS * IDX_DIM, IDX_DIM, IDX_HEADS,
               GROUP_WIDTH, GROUP_WIDTH, GROUP_WIDTH)
IN_COLS = sum(SPLIT_SIZES)

kernel_name = "hybrid_conv_pool_dsa_moba_trunk"


def rmsnorm(x, g):
    x32 = x.astype(jnp.float32)
    y = x32 * lax.rsqrt(jnp.mean(x32 * x32, axis=-1, keepdims=True) + RMS_EPS)
    return (y * g.astype(jnp.float32)).astype(x.dtype)


def rel_bucket(dist):
    n = jnp.maximum(dist, 0)
    max_exact = REL_BUCKETS // 2
    nf = jnp.maximum(n, 1).astype(jnp.float32)
    large = max_exact + (jnp.log(nf / max_exact) / math.log(REL_MAX_DIST / max_exact)
                         * (REL_BUCKETS - max_exact)).astype(jnp.int32)
    large = jnp.minimum(large, REL_BUCKETS - 1)
    return jnp.where(n < max_exact, n, large)


def short_conv_mixer(a_in, gate_c, gate_b, conv_w):
    h = gate_c * a_in
    y = lax.conv_general_dilated(h, conv_w[:, None, :].astype(h.dtype), window_strides=(1,),
                                 padding=[(CONV_WIDTH - 1, 0)],
                                 dimension_numbers=("NWC", "WIO", "NWC"),
                                 feature_group_count=h.shape[-1])
    return gate_b * y


def pool_mixer(v, pool_w, pool_scale):
    b, s, _ = v.shape
    v32 = v.astype(jnp.float32)
    cs = jnp.concatenate([jnp.zeros((b, 1, GROUP_WIDTH), jnp.float32), lax.cumsum(v32, axis=1)], axis=1)
    t = jnp.arange(s)
    outs = []
    for g, w in enumerate(POOL_WINDOWS):
        sl = slice(g * POOL_GROUP, (g + 1) * POOL_GROUP)
        c = cs[:, :, sl]
        upper = c[:, 1:]
        lower = jnp.concatenate([jnp.zeros((b, w - 1, POOL_GROUP), jnp.float32), c[:, :s + 1 - w]], axis=1)
        cnt = jnp.minimum(t + 1, w).astype(jnp.float32)[None, :, None]
        outs.append((upper - lower) / cnt - v32[:, :, sl])
    d = jnp.stack(outs, axis=2)
    y = jnp.einsum("bsgc,gcd->bsgd", d, pool_w.astype(jnp.float32)).reshape(b, s, GROUP_WIDTH)
    return (y * pool_scale.astype(jnp.float32)).astype(v.dtype)


def dsa_mixer(q, k, v, iq, ik, iw, bias_tab):
    b, s, nh, hd = q.shape
    topk = min(DSA_TOPK_MAX, s // 4)
    idx_scale = (IDX_HEADS ** -0.5) * (IDX_DIM ** -0.5)
    ik32 = ik.astype(jnp.float32)
    bidx = jnp.arange(b)[:, None, None]
    spos = jnp.arange(s)

    def chunk(c):
        t0 = c * Q_BLOCK
        tq = t0 + jnp.arange(Q_BLOCK)
        qc = lax.dynamic_slice_in_dim(q, t0, Q_BLOCK, axis=1).astype(jnp.float32)
        iqc = lax.dynamic_slice_in_dim(iq, t0, Q_BLOCK, axis=1).astype(jnp.float32)
        iwc = lax.dynamic_slice_in_dim(iw, t0, Q_BLOCK, axis=1).astype(jnp.float32)
        sc = jax.nn.relu(jnp.einsum("bqhd,bsd->bqhs", iqc, ik32))
        score = jnp.einsum("bqh,bqhs->bqs", iwc, sc) * idx_scale
        score = jnp.where(spos[None, None, :] <= tq[None, :, None], score, -jnp.inf)
        _, sel = lax.top_k(score, topk)
        valid = sel <= tq[None, :, None]
        kg = k[bidx, sel].astype(jnp.float32)
        vg = v[bidx, sel].astype(jnp.float32)
        logits = jnp.einsum("bqhd,bqkhd->bhqk", qc, kg) * (hd ** -0.5)
        bias = jnp.moveaxis(bias_tab.astype(jnp.float32)[:, rel_bucket(tq[None, :, None] - sel)], 0, 1)
        logits = jnp.where(valid[:, None], logits + bias, -jnp.inf)
        probs = jax.nn.softmax(logits, axis=-1)
        return jnp.einsum("bhqk,bqkhd->bqhd", probs, vg)

    out = lax.map(chunk, jnp.arange(s // Q_BLOCK))
    return out.transpose(1, 0, 2, 3, 4).reshape(b, s, nh * hd).astype(q.dtype)


def moba_mixer(q, k, v, bias_tab):
    b, s, nh, hd = q.shape
    nb = -(-s // MOBA_BLOCK)
    pad = nb * MOBA_BLOCK - s
    kp = jnp.pad(k, ((0, 0), (0, pad), (0, 0), (0, 0)))
    vp = jnp.pad(v, ((0, 0), (0, pad), (0, 0), (0, 0)))
    kblk = kp.reshape(b, nb, MOBA_BLOCK, nh, hd)
    kmean = jnp.mean(kblk.astype(jnp.float32), axis=2)
    kblk_t = kblk.transpose(0, 3, 1, 2, 4)
    vblk_t = vp.reshape(b, nb, MOBA_BLOCK, nh, hd).transpose(0, 3, 1, 2, 4)
    topb = min(MOBA_TOPB_MAX, nb - 1)
    scale = hd ** -0.5
    tab = bias_tab.astype(jnp.float32)
    bi = jnp.arange(b)[:, None, None, None]
    hi = jnp.arange(nh)[None, :, None, None]
    blk_off = jnp.arange(MOBA_BLOCK)

    def chunk(c):
        t0 = c * MOBA_Q_BLOCK
        tq = t0 + jnp.arange(MOBA_Q_BLOCK)
        ob = t0 // MOBA_BLOCK
        qc = lax.dynamic_slice_in_dim(q, t0, MOBA_Q_BLOCK, axis=1).astype(jnp.float32)
        ko = lax.dynamic_slice_in_dim(kp, ob * MOBA_BLOCK, MOBA_BLOCK, axis=1).astype(jnp.float32)
        vo = lax.dynamic_slice_in_dim(vp, ob * MOBA_BLOCK, MOBA_BLOCK, axis=1).astype(jnp.float32)
        spos_o = ob * MOBA_BLOCK + blk_off
        lo = jnp.einsum("bqhd,bshd->bhqs", qc, ko) * scale + tab[:, rel_bucket(tq[:, None] - spos_o[None, :])][None]
        lo = jnp.where((spos_o[None, :] <= tq[:, None])[None, None], lo, -jnp.inf)
        if topb == 0:
            po = jax.nn.softmax(lo, axis=-1)
            return jnp.einsum("bhqs,bshd->bqhd", po, vo)
        gate = jnp.einsum("bqhd,bnhd->bhqn", qc, kmean)
        gate = jnp.where(jnp.arange(nb)[None, None, None, :] < ob, gate, -jnp.inf)
        _, sel = lax.top_k(gate, topb)
        valid = sel < ob
        kg = kblk_t[bi, hi, sel].astype(jnp.float32)
        vg = vblk_t[bi, hi, sel].astype(jnp.float32)
        lp = jnp.einsum("bqhd,bhqnsd->bhqns", qc, kg) * scale
        spos_p = sel[..., None] * MOBA_BLOCK + blk_off
        lp = lp + tab[hi[..., None], rel_bucket(tq[None, None, :, None, None] - spos_p)]
        lp = jnp.where(valid[..., None], lp, -jnp.inf)
        npast = topb * MOBA_BLOCK
        logits = jnp.concatenate([lp.reshape(b, nh, MOBA_Q_BLOCK, npast), lo], axis=-1)
        probs = jax.nn.softmax(logits, axis=-1)
        pp = probs[..., :npast].reshape(b, nh, MOBA_Q_BLOCK, topb, MOBA_BLOCK)
        po = probs[..., npast:]
        return (jnp.einsum("bhqns,bhqnsd->bqhd", pp, vg) + jnp.einsum("bhqs,bshd->bqhd", po, vo))

    out = lax.map(chunk, jnp.arange(s // MOBA_Q_BLOCK))
    return out.transpose(1, 0, 2, 3, 4).reshape(b, s, nh * hd).astype(q.dtype)


def swiglu(h, w_gate_up, w_down):
    gu = h @ w_gate_up
    gate, up = jnp.split(gu, 2, axis=-1)
    return (jax.nn.silu(gate) * up) @ w_down


def setup_inputs(seed: int = 0) -> dict:
    key = jax.random.key(seed)
    ks = jax.random.split(key, 20)
    f32 = jnp.float32

    def nrm(k, shape, scale):
        return jax.random.normal(k, shape, f32) * scale

    def gain(k, shape):
        return 1.0 + 0.05 * jax.random.normal(k, shape, f32)

    return {
        "x": nrm(ks[0], (BATCH, SEQ, D_MODEL), 1.0),
        "p": nrm(ks[1], (DEPTH, BATCH, SEQ, PLE_DIM), 1.0),
        "rel_bias": nrm(ks[2], (N_ATTN_HEADS, REL_BUCKETS), 0.5),
        "g_mix_pre": gain(ks[3], (DEPTH, D_MODEL)),
        "w_in": nrm(ks[4], (DEPTH, D_MODEL, IN_COLS), D_MODEL ** -0.5),
        "conv_w": nrm(ks[5], (DEPTH, CONV_WIDTH, GROUP_WIDTH), CONV_WIDTH ** -0.5),
        "pool_w": nrm(ks[6], (DEPTH, len(POOL_WINDOWS), POOL_GROUP, POOL_GROUP), POOL_GROUP ** -0.5),
        "pool_scale": gain(ks[7], (DEPTH, GROUP_WIDTH)),
        "w_out": nrm(ks[8], (DEPTH, MIX_WIDTH, D_MODEL), MIX_WIDTH ** -0.5),
        "g_mix_post": gain(ks[9], (DEPTH, D_MODEL)),
        "g_ffn_pre": gain(ks[10], (DEPTH, D_MODEL)),
        "w_gate_up": nrm(ks[11], (DEPTH, D_MODEL, 2 * D_FF), D_MODEL ** -0.5),
        "w_down": nrm(ks[12], (DEPTH, D_FF, D_MODEL), D_FF ** -0.5),
        "g_ffn_post": gain(ks[13], (DEPTH, D_MODEL)),
        "g_ple": gain(ks[14], (DEPTH, D_MODEL)),
        "w_ple_gate": nrm(ks[15], (DEPTH, D_MODEL, D_MODEL), D_MODEL ** -0.5),
        "w_ple_proj": nrm(ks[16], (DEPTH, PLE_DIM, D_MODEL), PLE_DIM ** -0.5),
    }


def reference(x, p, rel_bias, g_mix_pre, w_in, conv_w, pool_w, pool_scale, w_out, g_mix_post,
              g_ffn_pre, w_gate_up, w_down, g_ffn_post, g_ple, w_ple_gate, w_ple_proj):
    b, s, _ = x.shape
    split_at = tuple(int(o) for o in np.cumsum(SPLIT_SIZES)[:-1])

    def heads(t):
        return t.reshape(b, s, -1, HEAD_DIM)

    for i in range(DEPTH):
        h = rmsnorm(x, g_mix_pre[i])
        z = h @ w_in[i]
        (a_in, a_c, a_b, pv, cq, ck, cv, iq, ik, iw, dq, dk, dv) = jnp.split(z, split_at, axis=-1)
        ya = short_conv_mixer(a_in, a_c, a_b, conv_w[i])
        yb = pool_mixer(pv, pool_w[i], pool_scale[i])
        yc = dsa_mixer(heads(cq), heads(ck), heads(cv), iq.reshape(b, s, IDX_HEADS, IDX_DIM), ik, iw,
                       rel_bias[:DSA_HEADS])
        yd = moba_mixer(heads(dq), heads(dk), heads(dv), rel_bias[DSA_HEADS:])
        mix = jnp.concatenate([ya, yb, yc, yd], axis=-1) @ w_out[i]
        x = x + rmsnorm(mix, g_mix_post[i])
        f = swiglu(rmsnorm(x, g_ffn_pre[i]), w_gate_up[i], w_down[i])
        x = x + rmsnorm(f, g_ffn_post[i])
        gate = jax.nn.sigmoid(rmsnorm(x, g_ple[i]) @ w_ple_gate[i])
        x = x + gate * (p[i] @ w_ple_proj[i])
    return x
```

```python
import functools
import math

import numpy as np
import jax
import jax.numpy as jnp
from jax import lax
from jax.experimental import pallas as pl
from jax.experimental.pallas import tpu as pltpu

HEAD_DIM = 64
GROUP_WIDTH = 256
CONV_WIDTH = 3
POOL_WINDOWS = (2, 4, 8, 16)
POOL_GROUP = GROUP_WIDTH // len(POOL_WINDOWS)
IDX_HEADS = 8
IDX_DIM = 64
DSA_TOPK_MAX = 256
MOBA_BLOCK = 256
MOBA_TOPB_MAX = 3
REL_BUCKETS = 32
REL_MAX_DIST = 128
N_HEADS = 8
RMS_EPS = 1e-6

LANES = 128
ATT_BLOCK = 256
HALO = 16
NEG = -1e30
INT_MIN = -(2 ** 31)
VMEM_LIMIT = 56 * 1024 * 1024

ZA_COLS = 4 * GROUP_WIDTH
ZB_COLS = 6 * GROUP_WIDTH + IDX_HEADS * IDX_DIM + LANES
ZC_COLS = LANES
IN_COLS_PADDED = ZA_COLS + ZB_COLS + ZC_COLS
COL_CHUNK = 256


def _rms(x, g):
    return x * lax.rsqrt(jnp.mean(x * x, axis=-1, keepdims=True) + RMS_EPS) * g


def _dot(a, b):
    return jnp.dot(a, b, preferred_element_type=jnp.float32)


def _dot_t(a, b):
    return lax.dot_general(a, b, (((1,), (1,)), ((), ())), preferred_element_type=jnp.float32)


def _inproj_kernel(x_ref, g_ref, w_ref, za_ref, zb_ref, zc_ref, kmean_ref):
    tm = x_ref.shape[0]
    h = _rms(x_ref[...], g_ref[...]).astype(jnp.bfloat16)
    for c0 in range(0, ZA_COLS, COL_CHUNK):
        za_ref[:, c0:c0 + COL_CHUNK] = _dot(h, w_ref[:, c0:c0 + COL_CHUNK])
    for b0 in range(0, ZB_COLS, COL_CHUNK):
        cw = min(COL_CHUNK, ZB_COLS - b0)
        z = _dot(h, w_ref[:, ZA_COLS + b0:ZA_COLS + b0 + cw])
        zb_ref[:, b0:b0 + cw] = z.astype(jnp.bfloat16)
        if b0 == 3 * GROUP_WIDTH:
            for r in range(tm // MOBA_BLOCK):
                blk = z[r * MOBA_BLOCK:(r + 1) * MOBA_BLOCK]
                kmean_ref[r] = jnp.mean(blk, axis=0, keepdims=True)
    zc_ref[...] = _dot(h, w_ref[:, ZA_COLS + ZB_COLS:IN_COLS_PADDED])


def _inproj(x2d, g, w):
    n, d = x2d.shape
    tm = 512
    return pl.pallas_call(
        _inproj_kernel,
        grid=(n // tm,),
        in_specs=[
            pl.BlockSpec((tm, d), lambda i: (i, 0)),
            pl.BlockSpec((1, d), lambda i: (0, 0)),
            pl.BlockSpec((d, IN_COLS_PADDED), lambda i: (0, 0), pipeline_mode=pl.Buffered(1)),
        ],
        out_specs=[
            pl.BlockSpec((tm, ZA_COLS), lambda i: (i, 0)),
            pl.BlockSpec((tm, ZB_COLS), lambda i: (i, 0)),
            pl.BlockSpec((tm, ZC_COLS), lambda i: (i, 0)),
            pl.BlockSpec((tm // MOBA_BLOCK, 1, GROUP_WIDTH), lambda i: (i, 0, 0)),
        ],
        out_shape=[
            jax.ShapeDtypeStruct((n, ZA_COLS), jnp.float32),
            jax.ShapeDtypeStruct((n, ZB_COLS), jnp.bfloat16),
            jax.ShapeDtypeStruct((n, ZC_COLS), jnp.float32),
            jax.ShapeDtypeStruct((n // MOBA_BLOCK, 1, GROUP_WIDTH), jnp.float32),
        ],
        compiler_params=pltpu.CompilerParams(
            dimension_semantics=("arbitrary",), vmem_limit_bytes=VMEM_LIMIT),
    )(x2d, g, w)


def _convpool_kernel(za_ref, cw_ref, pw_ref, ps_ref, y_ref, hbuf, vbuf):
    i = pl.program_id(1)
    ts = za_ref.shape[0]
    gw = GROUP_WIDTH

    @pl.when(i == 0)
    def _():
        hbuf[0:HALO, :] = jnp.zeros((HALO, gw), jnp.float32)
        vbuf[0:HALO, :] = jnp.zeros((HALO, gw), jnp.float32)

    @pl.when(i > 0)
    def _():
        hbuf[0:HALO, :] = hbuf[ts:ts + HALO, :]
        vbuf[0:HALO, :] = vbuf[ts:ts + HALO, :]

    a_in = za_ref[:, 0:gw]
    a_c = za_ref[:, gw:2 * gw]
    a_b = za_ref[:, 2 * gw:3 * gw]
    v = za_ref[:, 3 * gw:4 * gw]
    hbuf[HALO:HALO + ts, :] = a_c * a_in
    vbuf[HALO:HALO + ts, :] = v

    def hist(buf, d, lo, hi):
        return buf[HALO - d:HALO - d + ts, lo:hi]

    conv = (cw_ref[0:1, :] * hist(hbuf, 2, 0, gw) + cw_ref[1:2, :] * hist(hbuf, 1, 0, gw)
            + cw_ref[2:3, :] * hist(hbuf, 0, 0, gw))
    y_ref[:, 0:gw] = (a_b * conv).astype(y_ref.dtype)

    t_pos = i * ts + lax.broadcasted_iota(jnp.int32, (ts, LANES), 0)
    lane = lax.broadcasted_iota(jnp.int32, (ts, LANES), 1)
    first_group = lane < POOL_GROUP
    halves = []
    for half, (w_small, w_big) in enumerate(((2, 4), (8, 16))):
        lo, hi = half * LANES, (half + 1) * LANES
        acc = hist(vbuf, 0, lo, hi)
        for d in range(1, w_small):
            acc = acc + hist(vbuf, d, lo, hi)
        s_small = acc
        for d in range(w_small, w_big):
            acc = acc + hist(vbuf, d, lo, hi)
        wsum = jnp.where(first_group, s_small, acc)
        cnt = jnp.minimum(t_pos + 1, jnp.where(first_group, w_small, w_big)).astype(jnp.float32)
        halves.append(wsum / cnt - hist(vbuf, 0, lo, hi))
    dmat = jnp.concatenate(halves, axis=1).astype(jnp.bfloat16)
    yb = _dot(dmat, pw_ref[...]) * ps_ref[...]
    y_ref[:, gw:2 * gw] = yb.astype(y_ref.dtype)


def _convpool(za, conv_w, pool_w_bd, pool_scale, batch, seq):
    ts = 512
    nt = seq // ts
    return pl.pallas_call(
        _convpool_kernel,
        grid=(batch, nt),
        in_specs=[
            pl.BlockSpec((ts, ZA_COLS), lambda b, i: (b * nt + i, 0)),
            pl.BlockSpec((CONV_WIDTH, GROUP_WIDTH), lambda b, i: (0, 0)),
            pl.BlockSpec((GROUP_WIDTH, GROUP_WIDTH), lambda b, i: (0, 0)),
            pl.BlockSpec((1, GROUP_WIDTH), lambda b, i: (0, 0)),
        ],
        out_specs=pl.BlockSpec((ts, 2 * GROUP_WIDTH), lambda b, i: (b * nt + i, 0)),
        out_shape=jax.ShapeDtypeStruct((batch * seq, 2 * GROUP_WIDTH), jnp.bfloat16),
        scratch_shapes=[pltpu.VMEM((HALO + ts, GROUP_WIDTH), jnp.float32),
                        pltpu.VMEM((HALO + ts, GROUP_WIDTH), jnp.float32)],
        compiler_params=pltpu.CompilerParams(
            dimension_semantics=("arbitrary", "arbitrary"), vmem_limit_bytes=VMEM_LIMIT),
    )(za, conv_w, pool_w_bd, pool_scale)


def _attn_kernel(q_ref, k_ref, v_ref, iq_ref, ik_ref, iw_ref, kmean_ref, tdiag_ref, tprev_ref,
                 bfar_ref, o_ref, keys_scr, mb_scr, moba_scr, m_scr, l_scr, acc_scr, *, topk):
    qi = pl.program_id(1)
    blk = ATT_BLOCK
    hd = HEAD_DIM
    n_chunks = qi + 1
    row = lax.broadcasted_iota(jnp.int32, (blk, blk), 0)
    col = lax.broadcasted_iota(jnp.int32, (blk, blk), 1)
    causal = col <= row

    iq = iq_ref[...]
    iw = iw_ref[:, 0:IDX_HEADS]
    idx_scale = (IDX_HEADS ** -0.5) * (IDX_DIM ** -0.5)

    def score_keys(j):
        ik_c = ik_ref[pl.ds(pl.multiple_of(j * blk, blk), blk), 0:IDX_DIM]
        acc = jnp.zeros((blk, blk), jnp.float32)
        for h in range(IDX_HEADS):
            s = _dot_t(iq[:, h * IDX_DIM:(h + 1) * IDX_DIM], ik_c)
            acc = acc + jnp.maximum(s, 0.0) * iw[:, h:h + 1]
        score = acc * idx_scale + 0.0
        bits = pltpu.bitcast(score, jnp.int32)
        return jnp.where(bits < 0, bits ^ jnp.int32(0x7FFFFFFF), bits)

    def score_body(j, carry):
        keys_scr[j] = score_keys(j)
        return carry

    lax.fori_loop(0, qi, score_body, 0)
    keys_scr[qi] = jnp.where(causal, score_keys(qi), jnp.int32(INT_MIN))

    def count_ge(cand):
        def body(j, c):
            ge = jnp.where(keys_scr[j] >= cand, 1, 0)
            return c + ge[:, 0:LANES] + ge[:, LANES:2 * LANES]
        c = lax.fori_loop(0, n_chunks, body, jnp.zeros((blk, LANES), jnp.int32))
        return jnp.sum(c, axis=1, keepdims=True)

    def bit_body(it, carry):
        t_u, cnt_ge = carry
        cand_u = t_u | lax.shift_left(jnp.int32(1), 31 - it)
        cnt = count_ge(cand_u ^ jnp.int32(INT_MIN))
        take = cnt >= topk
        return jnp.where(take, cand_u, t_u), jnp.where(take, cnt, cnt_ge)

    t_u, cnt_ge = lax.fori_loop(
        0, 32, bit_body,
        (jnp.zeros((blk, 1), jnp.int32), jnp.full((blk, 1), n_chunks * blk, jnp.int32)))
    thr = t_u ^ jnp.int32(INT_MIN)
    tie_any = jnp.max(cnt_ge) > topk

    @pl.when(jnp.logical_not(tie_any))
    def _():
        def body(j, carry):
            mb_scr[j] = jnp.where(keys_scr[j] >= thr, 0.0, NEG)
            return carry
        lax.fori_loop(0, qi, body, 0)
        mb_scr[qi] = jnp.where((keys_scr[qi] >= thr) & causal, 0.0, NEG)

    @pl.when(tie_any)
    def _():
        def gt_body(j, c):
            gt = jnp.where(keys_scr[j] > thr, 1, 0)
            return c + gt[:, 0:LANES] + gt[:, LANES:2 * LANES]
        c = lax.fori_loop(0, n_chunks, gt_body, jnp.zeros((blk, LANES), jnp.int32))
        need = (topk - jnp.sum(c, axis=1, keepdims=True)).astype(jnp.float32)
        upper = jnp.where(row <= col, 1.0, 0.0).astype(jnp.bfloat16)

        def sel_chunk(j, base):
            kj = keys_scr[j]
            eq = kj == thr
            pref = _dot(jnp.where(eq, 1.0, 0.0).astype(jnp.bfloat16), upper) + base
            return (kj > thr) | (eq & (pref <= need)), pref[:, blk - 1:blk]

        def body(j, base):
            sel, base = sel_chunk(j, base)
            mb_scr[j] = jnp.where(sel, 0.0, NEG)
            return base
        base = lax.fori_loop(0, qi, body, jnp.zeros((blk, 1), jnp.float32))
        sel, _ = sel_chunk(qi, base)
        mb_scr[qi] = jnp.where(sel & causal, 0.0, NEG)

    q8 = q_ref[...]
    lane = lax.broadcasted_iota(jnp.int32, (blk, LANES), 1)
    lane_f = lane.astype(jnp.float32)
    nb = kmean_ref.shape[1]
    km = kmean_ref[0]
    km_hi = km.astype(jnp.bfloat16)
    km_lo = (km - km_hi.astype(jnp.float32)).astype(jnp.bfloat16)
    pad = jnp.zeros((LANES - nb, km.shape[1]), jnp.bfloat16)
    km_hi = jnp.concatenate([km_hi, pad], axis=0)
    km_lo = jnp.concatenate([km_lo, pad], axis=0)
    for h in range(4):
        sl = slice((4 + h) * hd, (5 + h) * hd)
        qh = q8[:, sl]
        gate = _dot_t(qh, km_hi[:, h * hd:(h + 1) * hd]) + _dot_t(qh, km_lo[:, h * hd:(h + 1) * hd])
        gate = jnp.where(lane < qi, gate, -jnp.inf)
        chosen = jnp.zeros((blk, LANES), jnp.bool_)
        for _ in range(MOBA_TOPB_MAX):
            mx = jnp.max(gate, axis=1, keepdims=True)
            is_mx = (gate == mx) & (gate > -jnp.inf)
            first = jnp.min(jnp.where(is_mx, lane_f, float(LANES)), axis=1, keepdims=True)
            pick = lane_f == first
            chosen = chosen | pick
            gate = jnp.where(pick, -jnp.inf, gate)
        moba_scr[h] = jnp.where(chosen, 0.0, NEG)

    m_scr[...] = jnp.full(m_scr.shape, NEG, jnp.float32)
    l_scr[...] = jnp.zeros(l_scr.shape, jnp.float32)
    acc_scr[...] = jnp.zeros(acc_scr.shape, jnp.float32)

    def attend(j, kind):
        start = pl.multiple_of(j * blk, blk)
        kc = k_ref[pl.ds(start, blk), :]
        vc = v_ref[pl.ds(start, blk), :]
        mbj = mb_scr[j]
        for h in range(N_HEADS):
            sl = slice(h * hd, (h + 1) * hd)
            s = _dot_t(q8[:, sl], kc[:, sl])
            if kind == "far":
                s = s + bfar_ref[0, h]
            elif kind == "prev":
                s = s + tprev_ref[h]
            else:
                s = s + tdiag_ref[h]
            if h < 4:
                s = s + mbj
            elif kind == "diag":
                s = jnp.where(causal, s, NEG)
            else:
                s = s + jnp.sum(jnp.where(lane == j, moba_scr[h - 4], 0.0), axis=1, keepdims=True)
            m_prev = m_scr[h]
            m_new = jnp.maximum(m_prev, jnp.max(s, axis=1, keepdims=True))
            alpha = jnp.exp(m_prev - m_new)
            p = jnp.exp(s - m_new)
            l_scr[h] = alpha * l_scr[h] + jnp.sum(p, axis=1, keepdims=True)
            acc_scr[:, sl] = alpha * acc_scr[:, sl] + _dot(p.astype(jnp.bfloat16), vc[:, sl])
            m_scr[h] = m_new

    def far_body(j, carry):
        attend(j, "far")
        return carry

    lax.fori_loop(0, qi - 1, far_body, 0)

    @pl.when(qi >= 1)
    def _():
        attend(qi - 1, "prev")

    attend(qi, "diag")

    for h in range(N_HEADS):
        sl = slice(h * hd, (h + 1) * hd)
        o_ref[:, sl] = (acc_scr[:, sl] / l_scr[h]).astype(o_ref.dtype)


def _attention(zb, zc, kmean, tdiag, tprev, bfar, batch, seq):
    blk = ATT_BLOCK
    nq = seq // blk
    topk = min(DSA_TOPK_MAX, seq // 4)
    w8 = N_HEADS * HEAD_DIM
    kernel = functools.partial(_attn_kernel, topk=topk)
    resident = dict(pipeline_mode=pl.Buffered(1))
    return pl.pallas_call(
        kernel,
        grid=(batch, nq),
        in_specs=[
            pl.BlockSpec((blk, w8), lambda b, i: (b * nq + i, 0)),
            pl.BlockSpec((seq, w8), lambda b, i: (b, 1)),
            pl.BlockSpec((seq, w8), lambda b, i: (b, 2)),
            pl.BlockSpec((blk, w8), lambda b, i: (b * nq + i, 3)),
            pl.BlockSpec((seq, LANES), lambda b, i: (b, 4 * w8 // LANES)),
            pl.BlockSpec((blk, LANES), lambda b, i: (b * nq + i, 0)),
            pl.BlockSpec((1, kmean.shape[1], GROUP_WIDTH), lambda b, i: (b, 0, 0)),
            pl.BlockSpec((N_HEADS, blk, blk), lambda b, i: (0, 0, 0), **resident),
            pl.BlockSpec((N_HEADS, blk, blk), lambda b, i: (0, 0, 0), **resident),
            pl.BlockSpec(memory_space=pltpu.SMEM),
        ],
        out_specs=pl.BlockSpec((blk, w8), lambda b, i: (b * nq + i, 0)),
        out_shape=jax.ShapeDtypeStruct((batch * seq, w8), jnp.bfloat16),
        scratch_shapes=[
            pltpu.VMEM((nq, blk, blk), jnp.int32),
            pltpu.VMEM((nq, blk, blk), jnp.float32),
            pltpu.VMEM((4, blk, LANES), jnp.float32),
            pltpu.VMEM((N_HEADS, blk, 1), jnp.float32),
            pltpu.VMEM((N_HEADS, blk, 1), jnp.float32),
            pltpu.VMEM((blk, w8), jnp.float32),
        ],
        compiler_params=pltpu.CompilerParams(
            dimension_semantics=("arbitrary", "arbitrary"), vmem_limit_bytes=VMEM_LIMIT),
    )(zb, zb, zb, zb, zb, zc, kmean, tdiag, tprev, bfar)


def _tail_kernel(x_ref, yab_ref, ycd_ref, p_ref, wo_ref, gpost_ref, gfpre_ref, wg_ref, wu_ref,
                 wd_ref, gfpost_ref, gple_ref, wpg_ref, wpp_ref, o_ref, f_scr):
    half = yab_ref.shape[1]
    mix = _dot(yab_ref[...], wo_ref[0:half, :]) + _dot(ycd_ref[...], wo_ref[half:2 * half, :])
    x1 = x_ref[...] + _rms(mix, gpost_ref[...])
    h2 = _rms(x1, gfpre_ref[...]).astype(jnp.bfloat16)

    f_scr[...] = jnp.zeros(f_scr.shape, jnp.float32)

    def ffn_body(c, carry):
        gate = _dot(h2, wg_ref[c])
        up = _dot(h2, wu_ref[c])
        act = (gate * jax.nn.sigmoid(gate) * up).astype(jnp.bfloat16)
        f_scr[...] += _dot(act, wd_ref[c])
        return carry

    lax.fori_loop(0, wg_ref.shape[0], ffn_body, 0)
    x2 = x1 + _rms(f_scr[...], gfpost_ref[...])
    hg = _rms(x2, gple_ref[...]).astype(jnp.bfloat16)
    gate = jax.nn.sigmoid(_dot(hg, wpg_ref[...]))
    o_ref[...] = x2 + gate * _dot(p_ref[...].astype(jnp.bfloat16), wpp_ref[...])


def _tail(x2d, yab, ycd, p2d, wo, gpost, gfpre, wg, wu, wd, gfpost, gple, wpg, wpp):
    n, d = x2d.shape
    tm = 512
    row = lambda i: (i, 0)
    const2 = lambda i: (0, 0)
    const3 = lambda i: (0, 0, 0)
    resident = dict(pipeline_mode=pl.Buffered(1))
    vec = pl.BlockSpec((1, d), const2)
    return pl.pallas_call(
        _tail_kernel,
        grid=(n // tm,),
        in_specs=[
            pl.BlockSpec((tm, d), row),
            pl.BlockSpec((tm, yab.shape[1]), row),
            pl.BlockSpec((tm, ycd.shape[1]), row),
            pl.BlockSpec((tm, p2d.shape[1]), row),
            pl.BlockSpec(wo.shape, const2, **resident),
            vec, vec,
            pl.BlockSpec(wg.shape, const3, **resident),
            pl.BlockSpec(wu.shape, const3, **resident),
            pl.BlockSpec(wd.shape, const3, **resident),
            vec, vec,
            pl.BlockSpec(wpg.shape, const2, **resident),
            pl.BlockSpec(wpp.shape, const2, **resident),
        ],
        out_specs=pl.BlockSpec((tm, d), row),
        out_shape=jax.ShapeDtypeStruct((n, d), jnp.float32),
        scratch_shapes=[pltpu.VMEM((tm, d), jnp.float32)],
        compiler_params=pltpu.CompilerParams(
            dimension_semantics=("arbitrary",), vmem_limit_bytes=VMEM_LIMIT),
    )(x2d, yab, ycd, p2d, wo, gpost, gfpre, wg, wu, wd, gfpost, gple, wpg, wpp)


def _rel_bucket_np(dist):
    n = np.maximum(dist, 0)
    max_exact = REL_BUCKETS // 2
    nf = np.maximum(n, 1).astype(np.float32)
    large = max_exact + (np.log(nf / np.float32(max_exact)) / np.float32(math.log(REL_MAX_DIST / max_exact))
                         * np.float32(REL_BUCKETS - max_exact)).astype(np.int32)
    large = np.minimum(large, REL_BUCKETS - 1)
    return np.where(n < max_exact, n, large)


def _bias_tables(rel_bias, seq):
    blk = ATT_BLOCK
    r = np.arange(blk)[:, None]
    c = np.arange(blk)[None, :]
    bucket_diag = _rel_bucket_np(r - c)
    bucket_prev = _rel_bucket_np(r - c + blk)
    far = _rel_bucket_np(np.arange(blk + 1, max(seq, blk + 2)))
    assert (far == far[0]).all(), "bias must be constant beyond the previous chunk"
    tab = rel_bias.astype(jnp.float32)
    return tab[:, bucket_diag], tab[:, bucket_prev], tab[:, int(far[0])][None, :]


def _regroup_w_in(w_in):
    gw = GROUP_WIDTH
    sizes = (gw, gw, gw, gw, gw, gw, gw, IDX_HEADS * IDX_DIM, IDX_DIM, IDX_HEADS, gw, gw, gw)
    offs = np.concatenate([[0], np.cumsum(sizes)])
    a_in, a_c, a_b, pv, cq, ck, cv, iq, ik, iw, dq, dk, dv = (
        w_in[..., offs[i]:offs[i + 1]] for i in range(len(sizes)))
    zeros = lambda w: jnp.zeros(w_in.shape[:-1] + (w,), w_in.dtype)
    cols = [a_in, a_c, a_b, pv,
            cq, dq, ck, dk, cv, dv, iq, ik, zeros(LANES - IDX_DIM),
            iw, zeros(LANES - IDX_HEADS)]
    return jnp.concatenate(cols, axis=-1).astype(jnp.bfloat16)


def _block_diag(pool_w):
    depth, ng, g, _ = pool_w.shape
    eye = jnp.eye(ng, dtype=pool_w.dtype)
    bd = jnp.einsum("lgcd,gh->lgchd", pool_w, eye).reshape(depth, ng * g, ng * g)
    return bd.astype(jnp.bfloat16)


@jax.jit
def kernel(x, p, rel_bias, g_mix_pre, w_in, conv_w, pool_w, pool_scale, w_out, g_mix_post, g_ffn_pre, w_gate_up, w_down, g_ffn_post, g_ple, w_ple_gate, w_ple_proj):
    batch, seq, d = x.shape
    depth = w_in.shape[0]
    n = batch * seq
    d_ff = w_down.shape[1]
    assert seq % 512 == 0 and d_ff % COL_CHUNK == 0
    nfc = d_ff // COL_CHUNK
    bf16 = jnp.bfloat16

    tdiag, tprev, bfar = _bias_tables(rel_bias, seq)
    w_in_r = _regroup_w_in(w_in)
    pool_bd = _block_diag(pool_w)
    qscale = np.ones((IN_COLS_PADDED,), np.float32)
    qscale[ZA_COLS:ZA_COLS + 2 * GROUP_WIDTH] = HEAD_DIM ** -0.5
    w_in_r = (w_in_r.astype(jnp.float32) * qscale).astype(bf16)
    wg = w_gate_up[:, :, :d_ff].reshape(depth, d, nfc, COL_CHUNK).transpose(0, 2, 1, 3).astype(bf16)
    wu = w_gate_up[:, :, d_ff:].reshape(depth, d, nfc, COL_CHUNK).transpose(0, 2, 1, 3).astype(bf16)
    wd = w_down.reshape(depth, nfc, COL_CHUNK, d).astype(bf16)
    wo = w_out.astype(bf16)
    wpg = w_ple_gate.astype(bf16)
    wpp = w_ple_proj.astype(bf16)

    x2d = x.reshape(n, d)
    for i in range(depth):
        za, zb, zc, kmean = _inproj(x2d, g_mix_pre[i][None, :], w_in_r[i])
        yab = _convpool(za, conv_w[i], pool_bd[i], pool_scale[i][None, :], batch, seq)
        ycd = _attention(zb, zc, kmean.reshape(batch, seq // MOBA_BLOCK, GROUP_WIDTH),
                         tdiag, tprev, bfar, batch, seq)
        x2d = _tail(x2d, yab, ycd, p[i].reshape(n, -1), wo[i], g_mix_post[i][None, :],
                    g_ffn_pre[i][None, :], wg[i], wu[i], wd[i], g_ffn_post[i][None, :],
                    g_ple[i][None, :], wpg[i], wpp[i])
    return x2d.reshape(batch, seq, d)
```

```python
import functools
import math

import numpy as np
import jax
import jax.numpy as jnp
from jax import lax
from jax.experimental import pallas as pl
from jax.experimental.pallas import tpu as pltpu

HEAD_DIM = 64
GROUP_WIDTH = 256
CONV_WIDTH = 3
POOL_WINDOWS = (2, 4, 8, 16)
POOL_GROUP = GROUP_WIDTH // len(POOL_WINDOWS)
IDX_HEADS = 8
IDX_DIM = 64
DSA_TOPK_MAX = 256
MOBA_BLOCK = 256
MOBA_TOPB_MAX = 3
REL_BUCKETS = 32
REL_MAX_DIST = 128
N_HEADS = 8
RMS_EPS = 1e-6

LANES = 128
SUBLANES = 8
ATT_BLOCK = 256
HALO = 16
NEG = -1e30
INT_MIN = -(2 ** 31)
VMEM_LIMIT = 56 * 1024 * 1024
ROW_TILE = 512

ZA_COLS = 4 * GROUP_WIDTH
W8 = N_HEADS * HEAD_DIM
ZB_COLS = 2 * W8 + IDX_HEADS * IDX_DIM + LANES
ZV_COLS = W8
ZC_COLS = LANES
IN_COLS_PADDED = ZA_COLS + ZB_COLS + ZV_COLS + ZC_COLS
COL_CHUNK = 256


def _rms(x, g):
    return x * lax.rsqrt(jnp.mean(x * x, axis=-1, keepdims=True) + RMS_EPS) * g


def _dot(a, b):
    return jnp.dot(a, b, preferred_element_type=jnp.float32)


def _dot_t(a, b):
    return lax.dot_general(a, b, (((1,), (1,)), ((), ())), preferred_element_type=jnp.float32)


def _inproj_kernel(x_ref, g_ref, w_ref, za_ref, zb_ref, vt_ref, zc_ref, kmean_ref):
    tm = x_ref.shape[0]
    blk = ATT_BLOCK
    h = _rms(x_ref[...], g_ref[...]).astype(jnp.bfloat16)
    for c0 in range(0, ZA_COLS, COL_CHUNK):
        za_ref[:, c0:c0 + COL_CHUNK] = _dot(h, w_ref[:, c0:c0 + COL_CHUNK])
    for b0 in range(0, ZB_COLS, COL_CHUNK):
        cw = min(COL_CHUNK, ZB_COLS - b0)
        z = _dot(h, w_ref[:, ZA_COLS + b0:ZA_COLS + b0 + cw])
        zb_ref[:, b0:b0 + cw] = z.astype(jnp.bfloat16)
        if b0 == W8 + GROUP_WIDTH:
            for r in range(tm // MOBA_BLOCK):
                kmean_ref[r] = jnp.mean(z[r * MOBA_BLOCK:(r + 1) * MOBA_BLOCK], axis=0, keepdims=True)
    v0 = ZA_COLS + ZB_COLS
    for c0 in range(0, ZV_COLS, COL_CHUNK):
        z = _dot(h, w_ref[:, v0 + c0:v0 + c0 + COL_CHUNK])
        for r in range(tm // blk):
            vt_ref[0, r, c0:c0 + COL_CHUNK, :] = z[r * blk:(r + 1) * blk, :].T.astype(jnp.bfloat16)
    zc_ref[...] = _dot(h, w_ref[:, v0 + ZV_COLS:IN_COLS_PADDED])


def _inproj(x2d, g, w, batch, seq):
    n, d = x2d.shape
    tm = ROW_TILE
    blk = ATT_BLOCK
    tiles_per_seq = seq // tm
    return pl.pallas_call(
        _inproj_kernel,
        grid=(n // tm,),
        in_specs=[
            pl.BlockSpec((tm, d), lambda i: (i, 0)),
            pl.BlockSpec((1, d), lambda i: (0, 0)),
            pl.BlockSpec((d, IN_COLS_PADDED), lambda i: (0, 0), pipeline_mode=pl.Buffered(1)),
        ],
        out_specs=[
            pl.BlockSpec((tm, ZA_COLS), lambda i: (i, 0)),
            pl.BlockSpec((tm, ZB_COLS), lambda i: (i, 0)),
            pl.BlockSpec((1, tm // blk, ZV_COLS, blk),
                         lambda i: (i // tiles_per_seq, i % tiles_per_seq, 0, 0)),
            pl.BlockSpec((tm, ZC_COLS), lambda i: (i, 0)),
            pl.BlockSpec((tm // MOBA_BLOCK, 1, GROUP_WIDTH), lambda i: (i, 0, 0)),
        ],
        out_shape=[
            jax.ShapeDtypeStruct((n, ZA_COLS), jnp.float32),
            jax.ShapeDtypeStruct((n, ZB_COLS), jnp.bfloat16),
            jax.ShapeDtypeStruct((batch, seq // blk, ZV_COLS, blk), jnp.bfloat16),
            jax.ShapeDtypeStruct((n, ZC_COLS), jnp.float32),
            jax.ShapeDtypeStruct((n // MOBA_BLOCK, 1, GROUP_WIDTH), jnp.float32),
        ],
        compiler_params=pltpu.CompilerParams(
            dimension_semantics=("arbitrary",), vmem_limit_bytes=VMEM_LIMIT),
    )(x2d, g, w)


def _convpool_kernel(za_ref, cw_ref, pw_ref, ps_ref, y_ref, hbuf, vbuf):
    i = pl.program_id(1)
    ts = za_ref.shape[0]
    gw = GROUP_WIDTH

    @pl.when(i == 0)
    def _():
        hbuf[0:HALO, :] = jnp.zeros((HALO, gw), jnp.float32)
        vbuf[0:HALO, :] = jnp.zeros((HALO, gw), jnp.float32)

    @pl.when(i > 0)
    def _():
        hbuf[0:HALO, :] = hbuf[ts:ts + HALO, :]
        vbuf[0:HALO, :] = vbuf[ts:ts + HALO, :]

    a_in = za_ref[:, 0:gw]
    a_c = za_ref[:, gw:2 * gw]
    a_b = za_ref[:, 2 * gw:3 * gw]
    v = za_ref[:, 3 * gw:4 * gw]
    hbuf[HALO:HALO + ts, :] = a_c * a_in
    vbuf[HALO:HALO + ts, :] = v

    def hist(buf, d, lo, hi):
        return buf[HALO - d:HALO - d + ts, lo:hi]

    conv = (cw_ref[0:1, :] * hist(hbuf, 2, 0, gw) + cw_ref[1:2, :] * hist(hbuf, 1, 0, gw)
            + cw_ref[2:3, :] * hist(hbuf, 0, 0, gw))
    y_ref[:, 0:gw] = (a_b * conv).astype(y_ref.dtype)

    t_pos = i * ts + lax.broadcasted_iota(jnp.int32, (ts, LANES), 0)
    lane = lax.broadcasted_iota(jnp.int32, (ts, LANES), 1)
    first_group = lane < POOL_GROUP
    halves = []
    for half, (w_small, w_big) in enumerate(((2, 4), (8, 16))):
        lo, hi = half * LANES, (half + 1) * LANES
        acc = hist(vbuf, 0, lo, hi)
        for d in range(1, w_small):
            acc = acc + hist(vbuf, d, lo, hi)
        s_small = acc
        for d in range(w_small, w_big):
            acc = acc + hist(vbuf, d, lo, hi)
        wsum = jnp.where(first_group, s_small, acc)
        cnt = jnp.minimum(t_pos + 1, jnp.where(first_group, w_small, w_big)).astype(jnp.float32)
        halves.append(wsum / cnt - hist(vbuf, 0, lo, hi))
    dmat = jnp.concatenate(halves, axis=1).astype(jnp.bfloat16)
    yb = _dot(dmat, pw_ref[...]) * ps_ref[...]
    y_ref[:, gw:2 * gw] = yb.astype(y_ref.dtype)


def _convpool(za, conv_w, pool_w_bd, pool_scale, batch, seq):
    ts = ROW_TILE
    nt = seq // ts
    return pl.pallas_call(
        _convpool_kernel,
        grid=(batch, nt),
        in_specs=[
            pl.BlockSpec((ts, ZA_COLS), lambda b, i: (b * nt + i, 0)),
            pl.BlockSpec((CONV_WIDTH, GROUP_WIDTH), lambda b, i: (0, 0)),
            pl.BlockSpec((GROUP_WIDTH, GROUP_WIDTH), lambda b, i: (0, 0)),
            pl.BlockSpec((1, GROUP_WIDTH), lambda b, i: (0, 0)),
        ],
        out_specs=pl.BlockSpec((ts, 2 * GROUP_WIDTH), lambda b, i: (b * nt + i, 0)),
        out_shape=jax.ShapeDtypeStruct((batch * seq, 2 * GROUP_WIDTH), jnp.bfloat16),
        scratch_shapes=[pltpu.VMEM((HALO + ts, GROUP_WIDTH), jnp.float32),
                        pltpu.VMEM((HALO + ts, GROUP_WIDTH), jnp.float32)],
        compiler_params=pltpu.CompilerParams(
            dimension_semantics=("arbitrary", "arbitrary"), vmem_limit_bytes=VMEM_LIMIT),
    )(za, conv_w, pool_w_bd, pool_scale)


def _attn_kernel(q_ref, k_ref, vt_ref, iq_ref, ik_ref, iw_ref, kmean_ref, tdiag_ref, tprev_ref,
                 o_ref, keys_scr, mb_scr, moba_scr, m_scr, l_scr, acc_scr, *, topk):
    qi = pl.program_id(1)
    blk = ATT_BLOCK
    hd = HEAD_DIM
    n_chunks = qi + 1
    key_idx = lax.broadcasted_iota(jnp.int32, (blk, blk), 0)
    qry_idx = lax.broadcasted_iota(jnp.int32, (blk, blk), 1)
    causal = key_idx <= qry_idx

    def fold(x):
        return jnp.sum(x.reshape(blk // SUBLANES, SUBLANES, blk), axis=0)

    iq = iq_ref[...]
    iw_t = iw_ref[...].T
    idx_scale = (IDX_HEADS ** -0.5) * (IDX_DIM ** -0.5)

    def score_keys(j):
        ik_c = ik_ref[pl.ds(pl.multiple_of(j * blk, blk), blk), 0:IDX_DIM]
        acc = jnp.zeros((blk, blk), jnp.float32)
        for h in range(IDX_HEADS):
            s = _dot_t(ik_c, iq[:, h * IDX_DIM:(h + 1) * IDX_DIM])
            acc = acc + jnp.maximum(s, 0.0) * iw_t[h:h + 1, :]
        score = acc * idx_scale + 0.0
        bits = pltpu.bitcast(score, jnp.int32)
        return jnp.where(bits < 0, bits ^ jnp.int32(0x7FFFFFFF), bits)

    def score_body(j, carry):
        keys_scr[j] = score_keys(j)
        return carry

    lax.fori_loop(0, qi, score_body, 0)
    keys_scr[qi] = jnp.where(causal, score_keys(qi), jnp.int32(INT_MIN))

    def count_keys(pred):
        def body(j, c):
            return c + fold(jnp.where(pred(keys_scr[j]), 1, 0))
        c = lax.fori_loop(0, n_chunks, body, jnp.zeros((SUBLANES, blk), jnp.int32))
        return jnp.sum(c, axis=0, keepdims=True)

    def bit_body(it, carry):
        t_u, cnt_ge = carry
        cand_u = t_u | lax.shift_left(jnp.int32(1), 31 - it)
        cand = cand_u ^ jnp.int32(INT_MIN)
        cnt = count_keys(lambda kj: kj >= cand)
        take = cnt >= topk
        return jnp.where(take, cand_u, t_u), jnp.where(take, cnt, cnt_ge)

    t_u, cnt_ge = lax.fori_loop(
        0, 32, bit_body,
        (jnp.zeros((1, blk), jnp.int32), jnp.full((1, blk), n_chunks * blk, jnp.int32)))
    thr = t_u ^ jnp.int32(INT_MIN)
    tie_any = jnp.max(cnt_ge) > topk

    @pl.when(jnp.logical_not(tie_any))
    def _():
        def body(j, carry):
            mb_scr[j] = jnp.where(keys_scr[j] >= thr, 0.0, NEG)
            return carry
        lax.fori_loop(0, qi, body, 0)
        mb_scr[qi] = jnp.where((keys_scr[qi] >= thr) & causal, 0.0, NEG)

    @pl.when(tie_any)
    def _():
        need = (topk - count_keys(lambda kj: kj > thr)).astype(jnp.float32)
        lower = jnp.where(qry_idx <= key_idx, 1.0, 0.0).astype(jnp.bfloat16)

        def sel_chunk(j, base):
            kj = keys_scr[j]
            eq = kj == thr
            pref = _dot(lower, jnp.where(eq, 1.0, 0.0).astype(jnp.bfloat16)) + base
            return (kj > thr) | (eq & (pref <= need)), pref[blk - 1:blk, :]

        def body(j, base):
            sel, base = sel_chunk(j, base)
            mb_scr[j] = jnp.where(sel, 0.0, NEG)
            return base
        base = lax.fori_loop(0, qi, body, jnp.zeros((1, blk), jnp.float32))
        sel, _ = sel_chunk(qi, base)
        mb_scr[qi] = jnp.where(sel & causal, 0.0, NEG)

    q8 = q_ref[...]
    nb = kmean_ref.shape[1]
    blk_idx = lax.broadcasted_iota(jnp.int32, (nb, blk), 0)
    blk_idx_f = blk_idx.astype(jnp.float32)
    km = kmean_ref[0]
    km_hi = km.astype(jnp.bfloat16)
    km_lo = (km - km_hi.astype(jnp.float32)).astype(jnp.bfloat16)
    for h in range(4):
        qh = q8[:, (4 + h) * hd:(5 + h) * hd]
        gate = _dot_t(km_hi[:, h * hd:(h + 1) * hd], qh) + _dot_t(km_lo[:, h * hd:(h + 1) * hd], qh)
        gate = jnp.where(blk_idx < qi, gate, -jnp.inf)
        chosen = jnp.zeros((nb, blk), jnp.bool_)
        for _ in range(MOBA_TOPB_MAX):
            mx = jnp.max(gate, axis=0, keepdims=True)
            is_mx = (gate == mx) & (gate > -jnp.inf)
            first = jnp.min(jnp.where(is_mx, blk_idx_f, float(nb)), axis=0, keepdims=True)
            pick = blk_idx_f == first
            chosen = chosen | pick
            gate = jnp.where(pick, -jnp.inf, gate)
        moba_scr[h] = jnp.where(chosen, 0.0, NEG)

    m_scr[...] = jnp.full(m_scr.shape, NEG, jnp.float32)
    l_scr[...] = jnp.zeros(l_scr.shape, jnp.float32)
    acc_scr[...] = jnp.zeros(acc_scr.shape, jnp.float32)

    def attend(j, kind):
        kc = k_ref[pl.ds(pl.multiple_of(j * blk, blk), blk), :]
        mbj = mb_scr[j]
        for h in range(N_HEADS):
            sl = slice(h * hd, (h + 1) * hd)
            s = _dot_t(kc[:, sl], q8[:, sl])
            if kind == "prev":
                s = s + tprev_ref[h]
            elif kind == "diag":
                s = s + tdiag_ref[h]
            if h < 4:
                s = s + mbj
            elif kind == "diag":
                s = jnp.where(causal, s, NEG)
            else:
                s = s + moba_scr[h - 4, pl.ds(j, 1), :]
            m_prev = m_scr[h:h + 1, :]
            m_new = jnp.maximum(m_prev, jnp.max(s, axis=0, keepdims=True))
            alpha = jnp.exp(m_prev - m_new)
            p = jnp.exp(s - m_new)
            l_scr[h:h + 1, :] = alpha * l_scr[h:h + 1, :] + jnp.sum(p, axis=0, keepdims=True)
            acc_scr[sl, :] = alpha * acc_scr[sl, :] + _dot(vt_ref[0, j, sl, :], p.astype(jnp.bfloat16))
            m_scr[h:h + 1, :] = m_new

    def far_body(j, carry):
        attend(j, "far")
        return carry

    lax.fori_loop(0, qi - 1, far_body, 0)

    @pl.when(qi >= 1)
    def _():
        attend(qi - 1, "prev")

    attend(qi, "diag")

    out_t = jnp.concatenate(
        [acc_scr[h * hd:(h + 1) * hd, :] / l_scr[h:h + 1, :] for h in range(N_HEADS)], axis=0)
    o_ref[...] = out_t.T.astype(o_ref.dtype)


def _attention(zb, vt, zc, kmean, tdiag, tprev, batch, seq):
    blk = ATT_BLOCK
    nq = seq // blk
    nb = kmean.shape[1]
    topk = min(DSA_TOPK_MAX, seq // 4)
    kernel = functools.partial(_attn_kernel, topk=topk)
    resident = dict(pipeline_mode=pl.Buffered(1))
    return pl.pallas_call(
        kernel,
        grid=(batch, nq),
        in_specs=[
            pl.BlockSpec((blk, W8), lambda b, i: (b * nq + i, 0)),
            pl.BlockSpec((seq, W8), lambda b, i: (b, 1)),
            pl.BlockSpec((1, nq, W8, blk), lambda b, i: (b, 0, 0, 0)),
            pl.BlockSpec((blk, W8), lambda b, i: (b * nq + i, 2)),
            pl.BlockSpec((seq, LANES), lambda b, i: (b, 3 * W8 // LANES)),
            pl.BlockSpec((blk, LANES), lambda b, i: (b * nq + i, 0)),
            pl.BlockSpec((1, nb, GROUP_WIDTH), lambda b, i: (b, 0, 0)),
            pl.BlockSpec((N_HEADS, blk, blk), lambda b, i: (0, 0, 0), **resident),
            pl.BlockSpec((N_HEADS, blk, blk), lambda b, i: (0, 0, 0), **resident),
        ],
        out_specs=pl.BlockSpec((blk, W8), lambda b, i: (b * nq + i, 0)),
        out_shape=jax.ShapeDtypeStruct((batch * seq, W8), jnp.bfloat16),
        scratch_shapes=[
            pltpu.VMEM((nq, blk, blk), jnp.int32),
            pltpu.VMEM((nq, blk, blk), jnp.float32),
            pltpu.VMEM((4, nb, blk), jnp.float32),
            pltpu.VMEM((N_HEADS, blk), jnp.float32),
            pltpu.VMEM((N_HEADS, blk), jnp.float32),
            pltpu.VMEM((W8, blk), jnp.float32),
        ],
        compiler_params=pltpu.CompilerParams(
            dimension_semantics=("arbitrary", "arbitrary"), vmem_limit_bytes=VMEM_LIMIT),
    )(zb, zb, vt, zb, zb, zc, kmean, tdiag, tprev)


def _tail_kernel(x_ref, yab_ref, ycd_ref, p_ref, wo_ref, gpost_ref, gfpre_ref, wg_ref, wu_ref,
                 wd_ref, gfpost_ref, gple_ref, wpg_ref, wpp_ref, o_ref, f_scr):
    half = yab_ref.shape[1]
    mix = _dot(yab_ref[...], wo_ref[0:half, :]) + _dot(ycd_ref[...], wo_ref[half:2 * half, :])
    x1 = x_ref[...] + _rms(mix, gpost_ref[...])
    h2 = _rms(x1, gfpre_ref[...]).astype(jnp.bfloat16)

    f_scr[...] = jnp.zeros(f_scr.shape, jnp.float32)

    def ffn_body(c, carry):
        gate = _dot(h2, wg_ref[c])
        up = _dot(h2, wu_ref[c])
        act = (gate * jax.nn.sigmoid(gate) * up).astype(jnp.bfloat16)
        f_scr[...] += _dot(act, wd_ref[c])
        return carry

    lax.fori_loop(0, wg_ref.shape[0], ffn_body, 0)
    x2 = x1 + _rms(f_scr[...], gfpost_ref[...])
    hg = _rms(x2, gple_ref[...]).astype(jnp.bfloat16)
    gate = jax.nn.sigmoid(_dot(hg, wpg_ref[...]))
    o_ref[...] = x2 + gate * _dot(p_ref[...].astype(jnp.bfloat16), wpp_ref[...])


def _tail(x2d, yab, ycd, p2d, wo, gpost, gfpre, wg, wu, wd, gfpost, gple, wpg, wpp):
    n, d = x2d.shape
    tm = ROW_TILE
    row = lambda i: (i, 0)
    const2 = lambda i: (0, 0)
    const3 = lambda i: (0, 0, 0)
    resident = dict(pipeline_mode=pl.Buffered(1))
    vec = pl.BlockSpec((1, d), const2)
    return pl.pallas_call(
        _tail_kernel,
        grid=(n // tm,),
        in_specs=[
            pl.BlockSpec((tm, d), row),
            pl.BlockSpec((tm, yab.shape[1]), row),
            pl.BlockSpec((tm, ycd.shape[1]), row),
            pl.BlockSpec((tm, p2d.shape[1]), row),
            pl.BlockSpec(wo.shape, const2, **resident),
            vec, vec,
            pl.BlockSpec(wg.shape, const3, **resident),
            pl.BlockSpec(wu.shape, const3, **resident),
            pl.BlockSpec(wd.shape, const3, **resident),
            vec, vec,
            pl.BlockSpec(wpg.shape, const2, **resident),
            pl.BlockSpec(wpp.shape, const2, **resident),
        ],
        out_specs=pl.BlockSpec((tm, d), row),
        out_shape=jax.ShapeDtypeStruct((n, d), jnp.float32),
        scratch_shapes=[pltpu.VMEM((tm, d), jnp.float32)],
        compiler_params=pltpu.CompilerParams(
            dimension_semantics=("arbitrary",), vmem_limit_bytes=VMEM_LIMIT),
    )(x2d, yab, ycd, p2d, wo, gpost, gfpre, wg, wu, wd, gfpost, gple, wpg, wpp)


def _rel_bucket_np(dist):
    n = np.maximum(dist, 0)
    max_exact = REL_BUCKETS // 2
    nf = np.maximum(n, 1).astype(np.float32)
    large = max_exact + (np.log(nf / np.float32(max_exact)) / np.float32(math.log(REL_MAX_DIST / max_exact))
                         * np.float32(REL_BUCKETS - max_exact)).astype(np.int32)
    large = np.minimum(large, REL_BUCKETS - 1)
    return np.where(n < max_exact, n, large)


def _bias_tables(rel_bias, seq):
    blk = ATT_BLOCK
    key = np.arange(blk)[:, None]
    qry = np.arange(blk)[None, :]
    far = _rel_bucket_np(np.arange(blk + 1, max(seq, blk + 2)))
    assert (far == far[0]).all(), "bias must be constant beyond the previous chunk"
    tab = rel_bias.astype(jnp.float32)
    tab = tab - tab[:, int(far[0])][:, None]

    def table(bucket):
        onehot = (jnp.asarray(bucket)[None] == jnp.arange(REL_BUCKETS)[:, None, None]).astype(jnp.float32)
        return jnp.einsum("hb,bkq->hkq", tab, onehot, precision=lax.Precision.HIGHEST)

    return table(_rel_bucket_np(qry - key)), table(_rel_bucket_np(qry - key + blk))


def _regroup_w_in(w_in):
    gw = GROUP_WIDTH
    sizes = (gw, gw, gw, gw, gw, gw, gw, IDX_HEADS * IDX_DIM, IDX_DIM, IDX_HEADS, gw, gw, gw)
    offs = np.concatenate([[0], np.cumsum(sizes)])
    a_in, a_c, a_b, pv, cq, ck, cv, iq, ik, iw, dq, dk, dv = (
        w_in[..., offs[i]:offs[i + 1]] for i in range(len(sizes)))
    zeros = lambda w: jnp.zeros(w_in.shape[:-1] + (w,), w_in.dtype)
    qs = HEAD_DIM ** -0.5
    cols = [a_in, a_c, a_b, pv,
            cq * qs, dq * qs, ck, dk, iq, ik, zeros(LANES - IDX_DIM),
            cv, dv,
            iw, zeros(LANES - IDX_HEADS)]
    return jnp.concatenate(cols, axis=-1).astype(jnp.bfloat16)


def _block_diag(pool_w):
    depth, ng, g, _ = pool_w.shape
    eye = jnp.eye(ng, dtype=pool_w.dtype)
    bd = jnp.einsum("lgcd,gh->lgchd", pool_w, eye).reshape(depth, ng * g, ng * g)
    return bd.astype(jnp.bfloat16)


@jax.jit
def kernel(x, p, rel_bias, g_mix_pre, w_in, conv_w, pool_w, pool_scale, w_out, g_mix_post, g_ffn_pre, w_gate_up, w_down, g_ffn_post, g_ple, w_ple_gate, w_ple_proj):
    batch, seq, d = x.shape
    depth = w_in.shape[0]
    n = batch * seq
    d_ff = w_down.shape[1]
    assert seq % ROW_TILE == 0 and d_ff % COL_CHUNK == 0
    nfc = d_ff // COL_CHUNK
    bf16 = jnp.bfloat16

    tdiag, tprev = _bias_tables(rel_bias, seq)
    w_in_r = _regroup_w_in(w_in)
    pool_bd = _block_diag(pool_w)
    wg = w_gate_up[:, :, :d_ff].reshape(depth, d, nfc, COL_CHUNK).transpose(0, 2, 1, 3).astype(bf16)
    wu = w_gate_up[:, :, d_ff:].reshape(depth, d, nfc, COL_CHUNK).transpose(0, 2, 1, 3).astype(bf16)
    wd = w_down.reshape(depth, nfc, COL_CHUNK, d).astype(bf16)
    wo = w_out.astype(bf16)
    wpg = w_ple_gate.astype(bf16)
    wpp = w_ple_proj.astype(bf16)

    x2d = x.reshape(n, d)
    for i in range(depth):
        za, zb, vt, zc, kmean = _inproj(x2d, g_mix_pre[i][None, :], w_in_r[i], batch, seq)
        yab = _convpool(za, conv_w[i], pool_bd[i], pool_scale[i][None, :], batch, seq)
        ycd = _attention(zb, vt, zc, kmean.reshape(batch, seq // MOBA_BLOCK, GROUP_WIDTH),
                         tdiag, tprev, batch, seq)
        x2d = _tail(x2d, yab, ycd, p[i].reshape(n, -1), wo[i], g_mix_post[i][None, :],
                    g_ffn_pre[i][None, :], wg[i], wu[i], wd[i], g_ffn_post[i][None, :],
                    g_ple[i][None, :], wpg[i], wpp[i])
    return x2d.reshape(batch, seq, d)
```

```python
import functools
import math

import numpy as np
import jax
import jax.numpy as jnp
from jax import lax
from jax.experimental import pallas as pl
from jax.experimental.pallas import tpu as pltpu

HEAD_DIM = 64
GROUP_WIDTH = 256
CONV_WIDTH = 3
POOL_WINDOWS = (2, 4, 8, 16)
POOL_GROUP = GROUP_WIDTH // len(POOL_WINDOWS)
IDX_HEADS = 8
IDX_DIM = 64
DSA_TOPK_MAX = 256
MOBA_BLOCK = 256
MOBA_TOPB_MAX = 3
REL_BUCKETS = 32
REL_MAX_DIST = 128
N_HEADS = 8
RMS_EPS = 1e-6

LANES = 128
SUBLANES = 8
ATT_BLOCK = 256
FAR_GROUP = 3
HALO = 16
NEG = -1e30
LOG2E = math.log2(math.e)
ACC_ROWS = HEAD_DIM + 16
INT_MIN = -(2 ** 31)
VMEM_LIMIT = 56 * 1024 * 1024
ROW_TILE = 512

ZA_COLS = 4 * GROUP_WIDTH
W8 = N_HEADS * HEAD_DIM
ZB_COLS = 2 * W8 + IDX_HEADS * IDX_DIM + 2 * IDX_DIM
ZV_COLS = W8
ZC_COLS = LANES
IN_COLS_PADDED = ZA_COLS + ZB_COLS + ZV_COLS + ZC_COLS
COL_CHUNK = 256


def _rms(x, g):
    return x * lax.rsqrt(jnp.mean(x * x, axis=-1, keepdims=True) + RMS_EPS) * g


def _dot(a, b):
    return jnp.dot(a, b, preferred_element_type=jnp.float32)


def _dot_t(a, b):
    return lax.dot_general(a, b, (((1,), (1,)), ((), ())), preferred_element_type=jnp.float32)


def _inproj_kernel(x_ref, g_ref, w_ref, za_ref, zb_ref, vt_ref, zc_ref, kmean_ref):
    tm = x_ref.shape[0]
    blk = ATT_BLOCK
    h = _rms(x_ref[...], g_ref[...]).astype(jnp.bfloat16)
    for c0 in range(0, ZA_COLS, COL_CHUNK):
        za_ref[:, c0:c0 + COL_CHUNK] = _dot(h, w_ref[:, c0:c0 + COL_CHUNK])
    for b0 in range(0, ZB_COLS, COL_CHUNK):
        cw = min(COL_CHUNK, ZB_COLS - b0)
        z = _dot(h, w_ref[:, ZA_COLS + b0:ZA_COLS + b0 + cw])
        zb_ref[:, b0:b0 + cw] = z.astype(jnp.bfloat16)
        if b0 == W8 + GROUP_WIDTH:
            for r in range(tm // MOBA_BLOCK):
                kmean_ref[r] = jnp.mean(z[r * MOBA_BLOCK:(r + 1) * MOBA_BLOCK], axis=0, keepdims=True)
    v0 = ZA_COLS + ZB_COLS
    for c0 in range(0, ZV_COLS, COL_CHUNK):
        z = _dot(h, w_ref[:, v0 + c0:v0 + c0 + COL_CHUNK])
        for r in range(tm // blk):
            vt_ref[0, r, c0:c0 + COL_CHUNK, :] = z[r * blk:(r + 1) * blk, :].T.astype(jnp.bfloat16)
    zc_ref[...] = _dot(h, w_ref[:, v0 + ZV_COLS:IN_COLS_PADDED])


def _inproj(x2d, g, w, batch, seq):
    n, d = x2d.shape
    tm = ROW_TILE
    blk = ATT_BLOCK
    tiles_per_seq = seq // tm
    return pl.pallas_call(
        _inproj_kernel,
        grid=(n // tm,),
        in_specs=[
            pl.BlockSpec((tm, d), lambda i: (i, 0)),
            pl.BlockSpec((1, d), lambda i: (0, 0)),
            pl.BlockSpec((d, IN_COLS_PADDED), lambda i: (0, 0), pipeline_mode=pl.Buffered(1)),
        ],
        out_specs=[
            pl.BlockSpec((tm, ZA_COLS), lambda i: (i, 0)),
            pl.BlockSpec((tm, ZB_COLS), lambda i: (i, 0)),
            pl.BlockSpec((1, tm // blk, ZV_COLS, blk),
                         lambda i: (i // tiles_per_seq, i % tiles_per_seq, 0, 0)),
            pl.BlockSpec((tm, ZC_COLS), lambda i: (i, 0)),
            pl.BlockSpec((tm // MOBA_BLOCK, 1, GROUP_WIDTH), lambda i: (i, 0, 0)),
        ],
        out_shape=[
            jax.ShapeDtypeStruct((n, ZA_COLS), jnp.float32),
            jax.ShapeDtypeStruct((n, ZB_COLS), jnp.bfloat16),
            jax.ShapeDtypeStruct((batch, seq // blk, ZV_COLS, blk), jnp.bfloat16),
            jax.ShapeDtypeStruct((n, ZC_COLS), jnp.float32),
            jax.ShapeDtypeStruct((n // MOBA_BLOCK, 1, GROUP_WIDTH), jnp.float32),
        ],
        compiler_params=pltpu.CompilerParams(
            dimension_semantics=("arbitrary",), vmem_limit_bytes=VMEM_LIMIT),
    )(x2d, g, w)


def _convpool_kernel(za_ref, cw_ref, pw_ref, ps_ref, y_ref, hbuf, vbuf):
    i = pl.program_id(1)
    ts = za_ref.shape[0]
    gw = GROUP_WIDTH

    @pl.when(i == 0)
    def _():
        hbuf[0:HALO, :] = jnp.zeros((HALO, gw), jnp.float32)
        vbuf[0:HALO, :] = jnp.zeros((HALO, gw), jnp.float32)

    @pl.when(i > 0)
    def _():
        hbuf[0:HALO, :] = hbuf[ts:ts + HALO, :]
        vbuf[0:HALO, :] = vbuf[ts:ts + HALO, :]

    a_in = za_ref[:, 0:gw]
    a_c = za_ref[:, gw:2 * gw]
    a_b = za_ref[:, 2 * gw:3 * gw]
    v = za_ref[:, 3 * gw:4 * gw]
    hbuf[HALO:HALO + ts, :] = a_c * a_in
    vbuf[HALO:HALO + ts, :] = v

    def hist(buf, d, lo, hi):
        return buf[HALO - d:HALO - d + ts, lo:hi]

    conv = (cw_ref[0:1, :] * hist(hbuf, 2, 0, gw) + cw_ref[1:2, :] * hist(hbuf, 1, 0, gw)
            + cw_ref[2:3, :] * hist(hbuf, 0, 0, gw))
    y_ref[:, 0:gw] = (a_b * conv).astype(y_ref.dtype)

    t_pos = i * ts + lax.broadcasted_iota(jnp.int32, (ts, LANES), 0)
    lane = lax.broadcasted_iota(jnp.int32, (ts, LANES), 1)
    first_group = lane < POOL_GROUP
    halves = []
    for half, (w_small, w_big) in enumerate(((2, 4), (8, 16))):
        lo, hi = half * LANES, (half + 1) * LANES
        acc = hist(vbuf, 0, lo, hi)
        for d in range(1, w_small):
            acc = acc + hist(vbuf, d, lo, hi)
        s_small = acc
        for d in range(w_small, w_big):
            acc = acc + hist(vbuf, d, lo, hi)
        wsum = jnp.where(first_group, s_small, acc)
        cnt = jnp.minimum(t_pos + 1, jnp.where(first_group, w_small, w_big)).astype(jnp.float32)
        halves.append(wsum / cnt - hist(vbuf, 0, lo, hi))
    dmat = jnp.concatenate(halves, axis=1).astype(jnp.bfloat16)
    yb = _dot(dmat, pw_ref[...]) * ps_ref[...]
    y_ref[:, gw:2 * gw] = yb.astype(y_ref.dtype)


def _convpool(za, conv_w, pool_w_bd, pool_scale, batch, seq):
    ts = ROW_TILE
    nt = seq // ts
    return pl.pallas_call(
        _convpool_kernel,
        grid=(batch, nt),
        in_specs=[
            pl.BlockSpec((ts, ZA_COLS), lambda b, i: (b * nt + i, 0)),
            pl.BlockSpec((CONV_WIDTH, GROUP_WIDTH), lambda b, i: (0, 0)),
            pl.BlockSpec((GROUP_WIDTH, GROUP_WIDTH), lambda b, i: (0, 0)),
            pl.BlockSpec((1, GROUP_WIDTH), lambda b, i: (0, 0)),
        ],
        out_specs=pl.BlockSpec((ts, 2 * GROUP_WIDTH), lambda b, i: (b * nt + i, 0)),
        out_shape=jax.ShapeDtypeStruct((batch * seq, 2 * GROUP_WIDTH), jnp.bfloat16),
        scratch_shapes=[pltpu.VMEM((HALO + ts, GROUP_WIDTH), jnp.float32),
                        pltpu.VMEM((HALO + ts, GROUP_WIDTH), jnp.float32)],
        compiler_params=pltpu.CompilerParams(
            dimension_semantics=("arbitrary", "arbitrary"), vmem_limit_bytes=VMEM_LIMIT),
    )(za, conv_w, pool_w_bd, pool_scale)


def _attn_kernel(q_ref, k_ref, vt_ref, iq_ref, ik_ref, iw_ref, kmean_ref, tdiag_ref, tprev_ref,
                 o_ref, keys_scr, hi_scr, lo_scr, gmax_scr, mb_scr, moba_scr, qpad_scr, m_scr,
                 acc_scr, *, topk):
    qi = pl.program_id(1)
    blk = ATT_BLOCK
    hd = HEAD_DIM
    n_chunks = qi + 1
    key_idx = lax.broadcasted_iota(jnp.int32, (blk, blk), 0)
    qry_idx = lax.broadcasted_iota(jnp.int32, (blk, blk), 1)
    causal = key_idx <= qry_idx

    def fold(x):
        return jnp.sum(x.reshape(blk // SUBLANES, SUBLANES, blk), axis=0)

    iq = iq_ref[...]
    iw_t = iw_ref[...].T
    idx_scale = (IDX_HEADS ** -0.5) * (IDX_DIM ** -0.5)

    lane_q = lax.broadcasted_iota(jnp.int32, (blk, LANES), 1)

    def score_keys(j):
        ik2 = ik_ref[pl.ds(pl.multiple_of(j * blk, blk), blk), :]
        ik_half = [jnp.where(lane_q < IDX_DIM, ik2, jnp.zeros_like(ik2)),
                   jnp.where(lane_q >= IDX_DIM, ik2, jnp.zeros_like(ik2))]
        acc = jnp.zeros((blk, blk), jnp.float32)
        for h in range(IDX_HEADS):
            s = _dot_t(ik_half[h % 2], iq[:, (h // 2) * LANES:(h // 2 + 1) * LANES])
            acc = acc + jnp.maximum(s, 0.0) * iw_t[h:h + 1, :]
        return acc * idx_scale + 0.0

    def to_key(score):
        bits = pltpu.bitcast(score, jnp.int32)
        return jnp.where(bits < 0, bits ^ jnp.int32(0x7FFFFFFF), bits)

    def hi16(key):
        return lax.shift_right_arithmetic(key, 16).astype(jnp.int16)

    def lo16(key):
        return ((key & 0xFFFF) - 0x8000).astype(jnp.int16)

    def store_keys(j, key):
        keys_scr[j] = key
        hi_scr[j] = hi16(key)
        lo_scr[j] = lo16(key)

    def score_body(j, carry):
        score = score_keys(j)
        store_keys(j, to_key(score))
        gmax_scr[...] = jnp.maximum(gmax_scr[...], score)
        return carry

    gmax_scr[...] = jnp.full((blk, blk), -jnp.inf, jnp.float32)
    lax.fori_loop(0, qi, score_body, 0)
    score = jnp.where(causal, score_keys(qi), -jnp.inf)
    store_keys(qi, jnp.where(causal, to_key(score), jnp.int32(INT_MIN)))
    gmax = jnp.maximum(gmax_scr[...], score)

    assert blk >= topk
    slot_min = jnp.min(gmax, axis=0, keepdims=True)
    lo_u = jnp.where(slot_min == -jnp.inf, jnp.int32(INT_MIN), to_key(slot_min)) ^ jnp.int32(INT_MIN)
    hi_u = to_key(jnp.max(gmax, axis=0, keepdims=True)) ^ jnp.int32(INT_MIN)
    open_bits = 32 - lax.clz(lo_u ^ hi_u)
    n_bits = jnp.max(open_bits.astype(jnp.float32)).astype(jnp.int32)
    low_mask = jnp.where(n_bits == 0, jnp.int32(0),
                         lax.shift_right_logical(jnp.int32(-1), (32 - n_bits) & 31))

    def count_keys(pred):
        def body(j, c):
            return c + fold(jnp.where(pred(keys_scr[j]), 1, 0))
        c = lax.fori_loop(0, n_chunks, body, jnp.zeros((SUBLANES, blk), jnp.int32))
        return jnp.sum(c, axis=0, keepdims=True)

    half_rows = 2 * SUBLANES

    def count_halves(ref, pred):
        def body(j, c):
            ge = jnp.where(pred(ref[j]), jnp.int16(1), jnp.int16(0))
            parts = [ge[r * half_rows:(r + 1) * half_rows] for r in range(blk // half_rows)]
            while len(parts) > 1:
                parts = [a + b for a, b in zip(parts[::2], parts[1::2])]
            return c + parts[0]
        c = lax.fori_loop(0, n_chunks, body, jnp.zeros((half_rows, blk), jnp.int16))
        return jnp.sum(c.astype(jnp.int32), axis=0, keepdims=True)

    def search_bits(first_it, last_it, count_ge, carry):
        def bit_body(it, carry):
            t_u, cnt_ge = carry
            cand_u = t_u | lax.shift_left(jnp.int32(1), 31 - it)
            cnt = count_ge(cand_u ^ jnp.int32(INT_MIN))
            take = cnt >= topk
            return jnp.where(take, cand_u, t_u), jnp.where(take, cnt, cnt_ge)
        return lax.fori_loop(first_it, last_it, bit_body, carry)

    t_u0 = lo_u & ~low_mask
    cand0 = t_u0 ^ jnp.int32(INT_MIN)
    carry = (t_u0, count_keys(lambda kj: kj >= cand0))
    first_it = 32 - n_bits
    carry = search_bits(first_it, 16, lambda cand: count_halves(hi_scr, lambda v: v >= hi16(cand)), carry)
    thr_hi = hi16(carry[0] ^ jnp.int32(INT_MIN))
    above = count_halves(hi_scr, lambda v: v > thr_hi)

    def open_body(j, c):
        lo_scr[j] = jnp.where(hi_scr[j] == thr_hi, lo_scr[j], jnp.int16(-0x8000))
        return c
    lax.fori_loop(0, n_chunks, open_body, 0)
    t_u, cnt_ge = search_bits(
        jnp.maximum(first_it, 16), 32,
        lambda cand: above + count_halves(lo_scr, lambda v: v >= lo16(cand)), carry)
    thr = t_u ^ jnp.int32(INT_MIN)
    tie_any = jnp.max(cnt_ge) > topk

    @pl.when(jnp.logical_not(tie_any))
    def _():
        def body(j, carry):
            mb_scr[j] = jnp.where(keys_scr[j] >= thr, 0.0, NEG)
            return carry
        lax.fori_loop(0, qi, body, 0)
        mb_scr[qi] = jnp.where((keys_scr[qi] >= thr) & causal, 0.0, NEG)

    @pl.when(tie_any)
    def _():
        need = (topk - count_keys(lambda kj: kj > thr)).astype(jnp.float32)
        lower = jnp.where(qry_idx <= key_idx, 1.0, 0.0).astype(jnp.bfloat16)

        def sel_chunk(j, base):
            kj = keys_scr[j]
            eq = kj == thr
            pref = _dot(lower, jnp.where(eq, 1.0, 0.0).astype(jnp.bfloat16)) + base
            return (kj > thr) | (eq & (pref <= need)), pref[blk - 1:blk, :]

        def body(j, base):
            sel, base = sel_chunk(j, base)
            mb_scr[j] = jnp.where(sel, 0.0, NEG)
            return base
        base = lax.fori_loop(0, qi, body, jnp.zeros((1, blk), jnp.float32))
        sel, _ = sel_chunk(qi, base)
        mb_scr[qi] = jnp.where(sel & causal, 0.0, NEG)

    q8 = q_ref[...]
    nb = kmean_ref.shape[1]
    blk_idx = lax.broadcasted_iota(jnp.int32, (nb, blk), 0)
    blk_idx_f = blk_idx.astype(jnp.float32)
    km = kmean_ref[0]
    km_hi = km.astype(jnp.bfloat16)
    km_lo = (km - km_hi.astype(jnp.float32)).astype(jnp.bfloat16)
    for h in range(4):
        qh = q8[:, (4 + h) * hd:(5 + h) * hd]
        gate = _dot_t(km_hi[:, h * hd:(h + 1) * hd], qh) + _dot_t(km_lo[:, h * hd:(h + 1) * hd], qh)
        gate = jnp.where(blk_idx < qi, gate, -jnp.inf)
        chosen = jnp.zeros((nb, blk), jnp.bool_)
        for _ in range(MOBA_TOPB_MAX):
            mx = jnp.max(gate, axis=0, keepdims=True)
            is_mx = (gate == mx) & (gate > -jnp.inf)
            first = jnp.min(jnp.where(is_mx, blk_idx_f, float(nb)), axis=0, keepdims=True)
            pick = blk_idx_f == first
            chosen = chosen | pick
            gate = jnp.where(pick, -jnp.inf, gate)
        moba_scr[h] = jnp.where(chosen, 0.0, NEG)

    m_scr[...] = jnp.full(m_scr.shape, NEG, jnp.float32)
    acc_scr[...] = jnp.zeros(acc_scr.shape, jnp.float32)
    ones_rows = jnp.ones((ACC_ROWS - hd, blk), jnp.bfloat16)

    for h in range(N_HEADS):
        pair = q8[:, (h // 2) * LANES:(h // 2 + 1) * LANES]
        qpad_scr[h] = jnp.where((lane_q >= hd) == (h % 2 == 1), pair, jnp.zeros_like(pair))

    def tile_rows(x, rows):
        return jnp.broadcast_to(x[None], (rows // SUBLANES, SUBLANES, blk)).reshape(rows, blk)

    def logits_phase(j, kind):
        kc = k_ref[pl.ds(pl.multiple_of(j * blk, blk), blk), :]
        staged = []
        for h in range(N_HEADS):
            s = _dot_t(kc[:, (h // 2) * LANES:(h // 2 + 1) * LANES], qpad_scr[h])
            if kind == "prev":
                s = s + tprev_ref[h]
            elif kind == "diag":
                s = s + tdiag_ref[h]
            if h < 4:
                s = s + mb_scr[j]
            elif kind == "diag":
                s = jnp.where(causal, s, NEG)
            mx = jnp.max(jnp.max(s.reshape(blk // SUBLANES, SUBLANES, blk), axis=0), axis=0, keepdims=True)
            m_prev = m_scr[h]
            m_new = jnp.maximum(m_prev, mx)
            shift = m_new
            if h >= 4 and kind != "diag":
                taken = moba_scr[h - 4, pl.ds(j, 1), :] == 0.0
                m_new = jnp.where(taken, m_new, m_prev)
                shift = jnp.where(taken, m_new, -NEG)
            m_scr[h] = m_new
            staged.append((s, shift, jnp.exp2(m_prev - m_new)))
        return staged

    def exp_phase(staged):
        return [(jnp.exp2(s - tile_rows(shift, blk)).astype(jnp.bfloat16), alpha)
                for s, shift, alpha in staged]

    def output_phase(j, probs):
        for h, (p, alpha) in enumerate(probs):
            vt_ones = jnp.concatenate([vt_ref[0, j, h * hd:(h + 1) * hd, :], ones_rows], axis=0)
            acc_scr[h] = tile_rows(alpha, ACC_ROWS) * acc_scr[h] + _dot(vt_ones, p)

    def attend(chunks):
        staged = [logits_phase(j, kind) for j, kind in chunks]
        probs = [exp_phase(st) for st in staged]
        for (j, _), pr in zip(chunks, probs):
            output_phase(j, pr)

    n_far = jnp.maximum(qi - 1, 0)

    def far_group(i, carry):
        attend([(FAR_GROUP * i + c, "far") for c in range(FAR_GROUP)])
        return carry

    lax.fori_loop(0, n_far // FAR_GROUP, far_group, 0)

    def far_single(j, carry):
        attend([(j, "far")])
        return carry

    lax.fori_loop(n_far - n_far % FAR_GROUP, n_far, far_single, 0)

    @pl.when(qi >= 1)
    def _():
        attend([(qi - 1, "prev"), (qi, "diag")])

    @pl.when(qi == 0)
    def _():
        attend([(qi, "diag")])

    out_t = jnp.concatenate(
        [acc_scr[h, 0:hd, :] / acc_scr[h, hd:hd + 1, :] for h in range(N_HEADS)], axis=0)
    o_ref[...] = out_t.T.astype(o_ref.dtype)


def _attention(zb, vt, zc, kmean, tdiag, tprev, batch, seq):
    blk = ATT_BLOCK
    nq = seq // blk
    nb = kmean.shape[1]
    topk = min(DSA_TOPK_MAX, seq // 4)
    kernel = functools.partial(_attn_kernel, topk=topk)
    resident = dict(pipeline_mode=pl.Buffered(1))
    return pl.pallas_call(
        kernel,
        grid=(batch, nq),
        in_specs=[
            pl.BlockSpec((blk, W8), lambda b, i: (b * nq + i, 0)),
            pl.BlockSpec((seq, W8), lambda b, i: (b, 1)),
            pl.BlockSpec((1, nq, W8, blk), lambda b, i: (b, 0, 0, 0)),
            pl.BlockSpec((blk, W8), lambda b, i: (b * nq + i, 2)),
            pl.BlockSpec((seq, LANES), lambda b, i: (b, 3 * W8 // LANES)),
            pl.BlockSpec((blk, LANES), lambda b, i: (b * nq + i, 0)),
            pl.BlockSpec((1, nb, GROUP_WIDTH), lambda b, i: (b, 0, 0)),
            pl.BlockSpec((N_HEADS, blk, blk), lambda b, i: (0, 0, 0), **resident),
            pl.BlockSpec((N_HEADS, blk, blk), lambda b, i: (0, 0, 0), **resident),
        ],
        out_specs=pl.BlockSpec((blk, W8), lambda b, i: (b * nq + i, 0)),
        out_shape=jax.ShapeDtypeStruct((batch * seq, W8), jnp.bfloat16),
        scratch_shapes=[
            pltpu.VMEM((nq, blk, blk), jnp.int32),
            pltpu.VMEM((nq, blk, blk), jnp.int16),
            pltpu.VMEM((nq, blk, blk), jnp.int16),
            pltpu.VMEM((blk, blk), jnp.float32),
            pltpu.VMEM((nq, blk, blk), jnp.float32),
            pltpu.VMEM((4, nb, blk), jnp.float32),
            pltpu.VMEM((N_HEADS, blk, LANES), jnp.bfloat16),
            pltpu.VMEM((N_HEADS, SUBLANES, blk), jnp.float32),
            pltpu.VMEM((N_HEADS, ACC_ROWS, blk), jnp.float32),
        ],
        compiler_params=pltpu.CompilerParams(
            dimension_semantics=("arbitrary", "arbitrary"), vmem_limit_bytes=VMEM_LIMIT),
    )(zb, zb, vt, zb, zb, zc, kmean, tdiag, tprev)


def _tail_kernel(x_ref, yab_ref, ycd_ref, p_ref, wo_ref, gpost_ref, gfpre_ref, wg_ref, wu_ref,
                 wd_ref, gfpost_ref, gple_ref, wpg_ref, wpp_ref, o_ref, f_scr):
    half = yab_ref.shape[1]
    mix = _dot(yab_ref[...], wo_ref[0:half, :]) + _dot(ycd_ref[...], wo_ref[half:2 * half, :])
    x1 = x_ref[...] + _rms(mix, gpost_ref[...])
    h2 = _rms(x1, gfpre_ref[...]).astype(jnp.bfloat16)

    f_scr[...] = jnp.zeros(f_scr.shape, jnp.float32)

    def ffn_body(c, carry):
        gate = _dot(h2, wg_ref[c])
        up = _dot(h2, wu_ref[c])
        act = (gate * jax.nn.sigmoid(gate) * up).astype(jnp.bfloat16)
        f_scr[...] += _dot(act, wd_ref[c])
        return carry

    lax.fori_loop(0, wg_ref.shape[0], ffn_body, 0)
    x2 = x1 + _rms(f_scr[...], gfpost_ref[...])
    hg = _rms(x2, gple_ref[...]).astype(jnp.bfloat16)
    gate = jax.nn.sigmoid(_dot(hg, wpg_ref[...]))
    o_ref[...] = x2 + gate * _dot(p_ref[...].astype(jnp.bfloat16), wpp_ref[...])


def _tail(x2d, yab, ycd, p2d, wo, gpost, gfpre, wg, wu, wd, gfpost, gple, wpg, wpp):
    n, d = x2d.shape
    tm = ROW_TILE
    row = lambda i: (i, 0)
    const2 = lambda i: (0, 0)
    const3 = lambda i: (0, 0, 0)
    resident = dict(pipeline_mode=pl.Buffered(1))
    vec = pl.BlockSpec((1, d), const2)
    return pl.pallas_call(
        _tail_kernel,
        grid=(n // tm,),
        in_specs=[
            pl.BlockSpec((tm, d), row),
            pl.BlockSpec((tm, yab.shape[1]), row),
            pl.BlockSpec((tm, ycd.shape[1]), row),
            pl.BlockSpec((tm, p2d.shape[1]), row),
            pl.BlockSpec(wo.shape, const2, **resident),
            vec, vec,
            pl.BlockSpec(wg.shape, const3, **resident),
            pl.BlockSpec(wu.shape, const3, **resident),
            pl.BlockSpec(wd.shape, const3, **resident),
            vec, vec,
            pl.BlockSpec(wpg.shape, const2, **resident),
            pl.BlockSpec(wpp.shape, const2, **resident),
        ],
        out_specs=pl.BlockSpec((tm, d), row),
        out_shape=jax.ShapeDtypeStruct((n, d), jnp.float32),
        scratch_shapes=[pltpu.VMEM((tm, d), jnp.float32)],
        compiler_params=pltpu.CompilerParams(
            dimension_semantics=("arbitrary",), vmem_limit_bytes=VMEM_LIMIT),
    )(x2d, yab, ycd, p2d, wo, gpost, gfpre, wg, wu, wd, gfpost, gple, wpg, wpp)


def _rel_bucket_np(dist):
    n = np.maximum(dist, 0)
    max_exact = REL_BUCKETS // 2
    nf = np.maximum(n, 1).astype(np.float32)
    large = max_exact + (np.log(nf / np.float32(max_exact)) / np.float32(math.log(REL_MAX_DIST / max_exact))
                         * np.float32(REL_BUCKETS - max_exact)).astype(np.int32)
    large = np.minimum(large, REL_BUCKETS - 1)
    return np.where(n < max_exact, n, large)


def _bias_tables(rel_bias, seq):
    blk = ATT_BLOCK
    key = np.arange(blk)[:, None]
    qry = np.arange(blk)[None, :]
    far = _rel_bucket_np(np.arange(blk + 1, max(seq, blk + 2)))
    assert (far == far[0]).all(), "bias must be constant beyond the previous chunk"
    tab = rel_bias.astype(jnp.float32)
    tab = (tab - tab[:, int(far[0])][:, None]) * LOG2E

    def table(bucket):
        onehot = (jnp.asarray(bucket)[None] == jnp.arange(REL_BUCKETS)[:, None, None]).astype(jnp.float32)
        return jnp.einsum("hb,bkq->hkq", tab, onehot, precision=lax.Precision.HIGHEST)

    return table(_rel_bucket_np(qry - key)), table(_rel_bucket_np(qry - key + blk))


def _regroup_w_in(w_in):
    gw = GROUP_WIDTH
    sizes = (gw, gw, gw, gw, gw, gw, gw, IDX_HEADS * IDX_DIM, IDX_DIM, IDX_HEADS, gw, gw, gw)
    offs = np.concatenate([[0], np.cumsum(sizes)])
    a_in, a_c, a_b, pv, cq, ck, cv, iq, ik, iw, dq, dk, dv = (
        w_in[..., offs[i]:offs[i + 1]] for i in range(len(sizes)))
    zeros = lambda w: jnp.zeros(w_in.shape[:-1] + (w,), w_in.dtype)
    qs = LOG2E * HEAD_DIM ** -0.5
    cols = [a_in, a_c, a_b, pv,
            cq * qs, dq * qs, ck, dk, iq, ik, ik,
            cv, dv,
            iw, zeros(LANES - IDX_HEADS)]
    return jnp.concatenate(cols, axis=-1).astype(jnp.bfloat16)


def _block_diag(pool_w):
    depth, ng, g, _ = pool_w.shape
    eye = jnp.eye(ng, dtype=pool_w.dtype)
    bd = jnp.einsum("lgcd,gh->lgchd", pool_w, eye).reshape(depth, ng * g, ng * g)
    return bd.astype(jnp.bfloat16)


@jax.jit
def kernel(x, p, rel_bias, g_mix_pre, w_in, conv_w, pool_w, pool_scale, w_out, g_mix_post, g_ffn_pre, w_gate_up, w_down, g_ffn_post, g_ple, w_ple_gate, w_ple_proj):
    batch, seq, d = x.shape
    depth = w_in.shape[0]
    n = batch * seq
    d_ff = w_down.shape[1]
    assert seq % ROW_TILE == 0 and d_ff % COL_CHUNK == 0
    nfc = d_ff // COL_CHUNK
    bf16 = jnp.bfloat16

    tdiag, tprev = _bias_tables(rel_bias, seq)
    w_in_r = _regroup_w_in(w_in)
    pool_bd = _block_diag(pool_w)
    wg = w_gate_up[:, :, :d_ff].reshape(depth, d, nfc, COL_CHUNK).transpose(0, 2, 1, 3).astype(bf16)
    wu = w_gate_up[:, :, d_ff:].reshape(depth, d, nfc, COL_CHUNK).transpose(0, 2, 1, 3).astype(bf16)
    wd = w_down.reshape(depth, nfc, COL_CHUNK, d).astype(bf16)
    wo = w_out.astype(bf16)
    wpg = w_ple_gate.astype(bf16)
    wpp = w_ple_proj.astype(bf16)

    x2d = x.reshape(n, d)
    for i in range(depth):
        za, zb, vt, zc, kmean = _inproj(x2d, g_mix_pre[i][None, :], w_in_r[i], batch, seq)
        yab = _convpool(za, conv_w[i], pool_bd[i], pool_scale[i][None, :], batch, seq)
        ycd = _attention(zb, vt, zc, kmean.reshape(batch, seq // MOBA_BLOCK, GROUP_WIDTH),
                         tdiag, tprev, batch, seq)
        x2d = _tail(x2d, yab, ycd, p[i].reshape(n, -1), wo[i], g_mix_post[i][None, :],
                    g_ffn_pre[i][None, :], wg[i], wu[i], wd[i], g_ffn_post[i][None, :],
                    g_ple[i][None, :], wpg[i], wpp[i])
    return x2d.reshape(batch, seq, d)
```

```python
import functools
import math

import numpy as np
import jax
import jax.numpy as jnp
from jax import lax
from jax.experimental import pallas as pl
from jax.experimental.pallas import tpu as pltpu

HEAD_DIM = 64
GROUP_WIDTH = 256
CONV_WIDTH = 3
POOL_WINDOWS = (2, 4, 8, 16)
POOL_GROUP = GROUP_WIDTH // len(POOL_WINDOWS)
IDX_HEADS = 8
IDX_DIM = 64
DSA_TOPK_MAX = 256
MOBA_BLOCK = 256
MOBA_TOPB_MAX = 3
REL_BUCKETS = 32
REL_MAX_DIST = 128
N_HEADS = 8
RMS_EPS = 1e-6

LANES = 128
SUBLANES = 8
ATT_BLOCK = 256
FAR_GROUP = 3
HALO = 16
NEG = -1e30
LOG2E = math.log2(math.e)
ACC_ROWS = HEAD_DIM + 16
INT_MIN = -(2 ** 31)
VMEM_LIMIT = 56 * 1024 * 1024
ROW_TILE = 512
TAIL_SUB = 256
FFN_CHUNK = 1024

ZA_COLS = 4 * GROUP_WIDTH
W8 = N_HEADS * HEAD_DIM
ZB_COLS = 2 * W8 + IDX_HEADS * IDX_DIM + 2 * IDX_DIM
ZV_COLS = W8
ZC_COLS = LANES
IN_COLS_PADDED = ZA_COLS + ZB_COLS + ZV_COLS + ZC_COLS
COL_CHUNK = 256


def _rms(x, g):
    return x * lax.rsqrt(jnp.mean(x * x, axis=-1, keepdims=True) + RMS_EPS) * g


def _dot(a, b):
    return jnp.dot(a, b, preferred_element_type=jnp.float32)


def _dot_t(a, b):
    return lax.dot_general(a, b, (((1,), (1,)), ((), ())), preferred_element_type=jnp.float32)


def _inproj_kernel(x_ref, g_ref, w_ref, cw_ref, pw_ref, ps_ref, yab_ref, zb_ref, vt_ref, zc_ref,
                   kmean_ref, hbuf, vbuf, *, tiles_per_seq):
    tm = x_ref.shape[0]
    blk = ATT_BLOCK
    h = _rms(x_ref[...], g_ref[...]).astype(jnp.bfloat16)
    a_in, a_c, a_b, pool_v = (_dot(h, w_ref[:, c0:c0 + GROUP_WIDTH])
                              for c0 in range(0, ZA_COLS, GROUP_WIDTH))
    _conv_pool(a_in, a_c, a_b, pool_v, pl.program_id(0) % tiles_per_seq,
               cw_ref, pw_ref, ps_ref, yab_ref, hbuf, vbuf)
    for b0 in range(0, ZB_COLS, COL_CHUNK):
        cw = min(COL_CHUNK, ZB_COLS - b0)
        z = _dot(h, w_ref[:, ZA_COLS + b0:ZA_COLS + b0 + cw])
        zb_ref[:, b0:b0 + cw] = z.astype(jnp.bfloat16)
        if b0 == W8 + GROUP_WIDTH:
            for r in range(tm // MOBA_BLOCK):
                kmean_ref[r] = jnp.mean(z[r * MOBA_BLOCK:(r + 1) * MOBA_BLOCK], axis=0, keepdims=True)
    v0 = ZA_COLS + ZB_COLS
    for c0 in range(0, ZV_COLS, COL_CHUNK):
        z = _dot(h, w_ref[:, v0 + c0:v0 + c0 + COL_CHUNK])
        for r in range(tm // blk):
            vt_ref[0, r, c0:c0 + COL_CHUNK, :] = z[r * blk:(r + 1) * blk, :].T.astype(jnp.bfloat16)
    zc_ref[...] = _dot(h, w_ref[:, v0 + ZV_COLS:IN_COLS_PADDED])


def _inproj(x2d, g, w, conv_w, pool_w_bd, pool_scale, batch, seq):
    n, d = x2d.shape
    tm = ROW_TILE
    blk = ATT_BLOCK
    tiles_per_seq = seq // tm
    const = lambda i: (0, 0)
    return pl.pallas_call(
        functools.partial(_inproj_kernel, tiles_per_seq=tiles_per_seq),
        grid=(n // tm,),
        in_specs=[
            pl.BlockSpec((tm, d), lambda i: (i, 0)),
            pl.BlockSpec((1, d), const),
            pl.BlockSpec((d, IN_COLS_PADDED), const, pipeline_mode=pl.Buffered(1)),
            pl.BlockSpec((CONV_WIDTH, GROUP_WIDTH), const),
            pl.BlockSpec((GROUP_WIDTH, GROUP_WIDTH), const),
            pl.BlockSpec((1, GROUP_WIDTH), const),
        ],
        out_specs=[
            pl.BlockSpec((tm, 2 * GROUP_WIDTH), lambda i: (i, 0)),
            pl.BlockSpec((tm, ZB_COLS), lambda i: (i, 0)),
            pl.BlockSpec((1, tm // blk, ZV_COLS, blk),
                         lambda i: (i // tiles_per_seq, i % tiles_per_seq, 0, 0)),
            pl.BlockSpec((tm, ZC_COLS), lambda i: (i, 0)),
            pl.BlockSpec((tm // MOBA_BLOCK, 1, GROUP_WIDTH), lambda i: (i, 0, 0)),
        ],
        out_shape=[
            jax.ShapeDtypeStruct((n, 2 * GROUP_WIDTH), jnp.bfloat16),
            jax.ShapeDtypeStruct((n, ZB_COLS), jnp.bfloat16),
            jax.ShapeDtypeStruct((batch, seq // blk, ZV_COLS, blk), jnp.bfloat16),
            jax.ShapeDtypeStruct((n, ZC_COLS), jnp.float32),
            jax.ShapeDtypeStruct((n // MOBA_BLOCK, 1, GROUP_WIDTH), jnp.float32),
        ],
        scratch_shapes=[pltpu.VMEM((HALO + tm, GROUP_WIDTH), jnp.float32),
                        pltpu.VMEM((HALO + tm, GROUP_WIDTH), jnp.float32)],
        compiler_params=pltpu.CompilerParams(
            dimension_semantics=("arbitrary",), vmem_limit_bytes=VMEM_LIMIT),
    )(x2d, g, w, conv_w, pool_w_bd, pool_scale)


def _conv_pool(a_in, a_c, a_b, v, i, cw_ref, pw_ref, ps_ref, y_ref, hbuf, vbuf):
    ts = a_in.shape[0]
    gw = GROUP_WIDTH

    @pl.when(i == 0)
    def _():
        hbuf[0:HALO, :] = jnp.zeros((HALO, gw), jnp.float32)
        vbuf[0:HALO, :] = jnp.zeros((HALO, gw), jnp.float32)

    @pl.when(i > 0)
    def _():
        hbuf[0:HALO, :] = hbuf[ts:ts + HALO, :]
        vbuf[0:HALO, :] = vbuf[ts:ts + HALO, :]

    hbuf[HALO:HALO + ts, :] = a_c * a_in
    vbuf[HALO:HALO + ts, :] = v

    def hist(buf, d, lo, hi):
        return buf[HALO - d:HALO - d + ts, lo:hi]

    conv = (cw_ref[0:1, :] * hist(hbuf, 2, 0, gw) + cw_ref[1:2, :] * hist(hbuf, 1, 0, gw)
            + cw_ref[2:3, :] * hist(hbuf, 0, 0, gw))
    y_ref[:, 0:gw] = (a_b * conv).astype(y_ref.dtype)

    t_pos = i * ts + lax.broadcasted_iota(jnp.int32, (ts, LANES), 0)
    lane = lax.broadcasted_iota(jnp.int32, (ts, LANES), 1)
    first_group = lane < POOL_GROUP
    halves = []
    for half, (w_small, w_big) in enumerate(((2, 4), (8, 16))):
        lo, hi = half * LANES, (half + 1) * LANES
        acc = hist(vbuf, 0, lo, hi)
        for d in range(1, w_small):
            acc = acc + hist(vbuf, d, lo, hi)
        s_small = acc
        for d in range(w_small, w_big):
            acc = acc + hist(vbuf, d, lo, hi)
        wsum = jnp.where(first_group, s_small, acc)
        cnt = jnp.minimum(t_pos + 1, jnp.where(first_group, w_small, w_big)).astype(jnp.float32)
        halves.append(wsum / cnt - hist(vbuf, 0, lo, hi))
    dmat = jnp.concatenate(halves, axis=1).astype(jnp.bfloat16)
    yb = _dot(dmat, pw_ref[...]) * ps_ref[...]
    y_ref[:, gw:2 * gw] = yb.astype(y_ref.dtype)


def _attn_kernel(q_ref, k_ref, vt_ref, iq_ref, ik_ref, iw_ref, kmean_ref, tdiag_ref, tprev_ref,
                 o_ref, keys_scr, hi_scr, lo_scr, gmax_scr, mb_scr, moba_scr, qpad_scr, m_scr,
                 acc_scr, *, topk):
    qi = pl.program_id(1)
    blk = ATT_BLOCK
    hd = HEAD_DIM
    n_chunks = qi + 1
    key_idx = lax.broadcasted_iota(jnp.int32, (blk, blk), 0)
    qry_idx = lax.broadcasted_iota(jnp.int32, (blk, blk), 1)
    causal = key_idx <= qry_idx

    def fold(x):
        return jnp.sum(x.reshape(blk // SUBLANES, SUBLANES, blk), axis=0)

    iq = iq_ref[...]
    iw_t = iw_ref[...].T
    idx_scale = (IDX_HEADS ** -0.5) * (IDX_DIM ** -0.5)

    lane_q = lax.broadcasted_iota(jnp.int32, (blk, LANES), 1)

    def score_keys(j):
        ik2 = ik_ref[pl.ds(pl.multiple_of(j * blk, blk), blk), :]
        ik_half = [jnp.where(lane_q < IDX_DIM, ik2, jnp.zeros_like(ik2)),
                   jnp.where(lane_q >= IDX_DIM, ik2, jnp.zeros_like(ik2))]
        acc = jnp.zeros((blk, blk), jnp.float32)
        for h in range(IDX_HEADS):
            s = _dot_t(ik_half[h % 2], iq[:, (h // 2) * LANES:(h // 2 + 1) * LANES])
            acc = acc + jnp.maximum(s, 0.0) * iw_t[h:h + 1, :]
        return acc * idx_scale + 0.0

    def to_key(score):
        bits = pltpu.bitcast(score, jnp.int32)
        return jnp.where(bits < 0, bits ^ jnp.int32(0x7FFFFFFF), bits)

    def hi16(key):
        return lax.shift_right_arithmetic(key, 16).astype(jnp.int16)

    def lo16(key):
        return ((key & 0xFFFF) - 0x8000).astype(jnp.int16)

    def store_keys(j, key):
        keys_scr[j] = key
        hi_scr[j] = hi16(key)
        lo_scr[j] = lo16(key)

    def score_body(j, carry):
        score = score_keys(j)
        store_keys(j, to_key(score))
        gmax_scr[...] = jnp.maximum(gmax_scr[...], score)
        return carry

    gmax_scr[...] = jnp.full((blk, blk), -jnp.inf, jnp.float32)

    def score_pair(i, carry):
        sa = score_keys(2 * i)
        sb = score_keys(2 * i + 1)
        store_keys(2 * i, to_key(sa))
        store_keys(2 * i + 1, to_key(sb))
        gmax_scr[...] = jnp.maximum(gmax_scr[...], jnp.maximum(sa, sb))
        return carry

    lax.fori_loop(0, qi // 2, score_pair, 0)
    lax.fori_loop(qi - qi % 2, qi, score_body, 0)
    score = jnp.where(causal, score_keys(qi), -jnp.inf)
    store_keys(qi, jnp.where(causal, to_key(score), jnp.int32(INT_MIN)))
    gmax = jnp.maximum(gmax_scr[...], score)

    assert blk >= topk
    slot_min = jnp.min(gmax, axis=0, keepdims=True)
    lo_u = jnp.where(slot_min == -jnp.inf, jnp.int32(INT_MIN), to_key(slot_min)) ^ jnp.int32(INT_MIN)
    hi_u = to_key(jnp.max(gmax, axis=0, keepdims=True)) ^ jnp.int32(INT_MIN)
    open_bits = 32 - lax.clz(lo_u ^ hi_u)
    n_bits = jnp.max(open_bits.astype(jnp.float32)).astype(jnp.int32)
    low_mask = jnp.where(n_bits == 0, jnp.int32(0),
                         lax.shift_right_logical(jnp.int32(-1), (32 - n_bits) & 31))

    def count_keys(pred):
        def body(j, c):
            return c + fold(jnp.where(pred(keys_scr[j]), 1, 0))
        c = lax.fori_loop(0, n_chunks, body, jnp.zeros((SUBLANES, blk), jnp.int32))
        return jnp.sum(c, axis=0, keepdims=True)

    half_rows = 2 * SUBLANES

    def count_halves(ref, pred):
        def body(j, c):
            ge = jnp.where(pred(ref[j]), jnp.int16(1), jnp.int16(0))
            parts = [ge[r * half_rows:(r + 1) * half_rows] for r in range(blk // half_rows)]
            while len(parts) > 1:
                parts = [a + b for a, b in zip(parts[::2], parts[1::2])]
            return c + parts[0]
        c = lax.fori_loop(0, n_chunks, body, jnp.zeros((half_rows, blk), jnp.int16))
        return jnp.sum(c.astype(jnp.int32), axis=0, keepdims=True)

    def search_bits(first_it, last_it, count_ge, carry):
        def bit_body(it, carry):
            t_u, cnt_ge = carry
            cand_u = t_u | lax.shift_left(jnp.int32(1), 31 - it)
            cnt = count_ge(cand_u ^ jnp.int32(INT_MIN))
            take = cnt >= topk
            return jnp.where(take, cand_u, t_u), jnp.where(take, cnt, cnt_ge)
        return lax.fori_loop(first_it, last_it, bit_body, carry)

    t_u0 = lo_u & ~low_mask
    cand0 = t_u0 ^ jnp.int32(INT_MIN)
    carry = (t_u0, count_keys(lambda kj: kj >= cand0))
    first_it = 32 - n_bits
    carry = search_bits(first_it, 16, lambda cand: count_halves(hi_scr, lambda v: v >= hi16(cand)), carry)
    thr_hi = hi16(carry[0] ^ jnp.int32(INT_MIN))
    above = count_halves(hi_scr, lambda v: v > thr_hi)

    def open_body(j, c):
        lo_scr[j] = jnp.where(hi_scr[j] == thr_hi, lo_scr[j], jnp.int16(-0x8000))
        return c
    lax.fori_loop(0, n_chunks, open_body, 0)
    t_u, cnt_ge = search_bits(
        jnp.maximum(first_it, 16), 32,
        lambda cand: above + count_halves(lo_scr, lambda v: v >= lo16(cand)), carry)
    thr = t_u ^ jnp.int32(INT_MIN)
    tie_any = jnp.max(cnt_ge) > topk

    @pl.when(jnp.logical_not(tie_any))
    def _():
        def body(j, carry):
            mb_scr[j] = jnp.where(keys_scr[j] >= thr, 0.0, NEG)
            return carry
        lax.fori_loop(0, qi, body, 0)
        mb_scr[qi] = jnp.where((keys_scr[qi] >= thr) & causal, 0.0, NEG)

    @pl.when(tie_any)
    def _():
        need = (topk - count_keys(lambda kj: kj > thr)).astype(jnp.float32)
        lower = jnp.where(qry_idx <= key_idx, 1.0, 0.0).astype(jnp.bfloat16)

        def sel_chunk(j, base):
            kj = keys_scr[j]
            eq = kj == thr
            pref = _dot(lower, jnp.where(eq, 1.0, 0.0).astype(jnp.bfloat16)) + base
            return (kj > thr) | (eq & (pref <= need)), pref[blk - 1:blk, :]

        def body(j, base):
            sel, base = sel_chunk(j, base)
            mb_scr[j] = jnp.where(sel, 0.0, NEG)
            return base
        base = lax.fori_loop(0, qi, body, jnp.zeros((1, blk), jnp.float32))
        sel, _ = sel_chunk(qi, base)
        mb_scr[qi] = jnp.where(sel & causal, 0.0, NEG)

    q8 = q_ref[...]
    nb = kmean_ref.shape[1]
    blk_idx = lax.broadcasted_iota(jnp.int32, (nb, blk), 0)
    blk_idx_f = blk_idx.astype(jnp.float32)
    km = kmean_ref[0]
    km_hi = km.astype(jnp.bfloat16)
    km_lo = (km - km_hi.astype(jnp.float32)).astype(jnp.bfloat16)
    for h in range(4):
        qh = q8[:, (4 + h) * hd:(5 + h) * hd]
        gate = _dot_t(km_hi[:, h * hd:(h + 1) * hd], qh) + _dot_t(km_lo[:, h * hd:(h + 1) * hd], qh)
        gate = jnp.where(blk_idx < qi, gate, -jnp.inf)
        chosen = jnp.zeros((nb, blk), jnp.bool_)
        for _ in range(MOBA_TOPB_MAX):
            mx = jnp.max(gate, axis=0, keepdims=True)
            is_mx = (gate == mx) & (gate > -jnp.inf)
            first = jnp.min(jnp.where(is_mx, blk_idx_f, float(nb)), axis=0, keepdims=True)
            pick = blk_idx_f == first
            chosen = chosen | pick
            gate = jnp.where(pick, -jnp.inf, gate)
        moba_scr[h] = jnp.where(chosen, 0.0, NEG)

    m_scr[...] = jnp.full(m_scr.shape, NEG, jnp.float32)
    acc_scr[...] = jnp.zeros(acc_scr.shape, jnp.float32)
    ones_rows = jnp.ones((ACC_ROWS - hd, blk), jnp.bfloat16)

    for h in range(N_HEADS):
        pair = q8[:, (h // 2) * LANES:(h // 2 + 1) * LANES]
        qpad_scr[h] = jnp.where((lane_q >= hd) == (h % 2 == 1), pair, jnp.zeros_like(pair))

    def tile_rows(x, rows):
        return jnp.broadcast_to(x[None], (rows // SUBLANES, SUBLANES, blk)).reshape(rows, blk)

    def logits_phase(j, kind):
        kc = k_ref[pl.ds(pl.multiple_of(j * blk, blk), blk), :]
        staged = []
        for h in range(N_HEADS):
            s = _dot_t(kc[:, (h // 2) * LANES:(h // 2 + 1) * LANES], qpad_scr[h])
            if kind == "prev":
                s = s + tprev_ref[h]
            elif kind == "diag":
                s = s + tdiag_ref[h]
            if h < 4:
                s = s + mb_scr[j]
            elif kind == "diag":
                s = jnp.where(causal, s, NEG)
            mx = jnp.max(jnp.max(s.reshape(blk // SUBLANES, SUBLANES, blk), axis=0), axis=0, keepdims=True)
            m_prev = m_scr[h]
            m_new = jnp.maximum(m_prev, mx)
            shift = m_new
            if h >= 4 and kind != "diag":
                taken = moba_scr[h - 4, pl.ds(j, 1), :] == 0.0
                m_new = jnp.where(taken, m_new, m_prev)
                shift = jnp.where(taken, m_new, -NEG)
            m_scr[h] = m_new
            staged.append((s, shift, jnp.exp2(m_prev - m_new)))
        return staged

    def exp_phase(staged):
        return [(jnp.exp2(s - tile_rows(shift, blk)).astype(jnp.bfloat16), alpha)
                for s, shift, alpha in staged]

    def output_phase(j, probs):
        for h, (p, alpha) in enumerate(probs):
            vt_ones = jnp.concatenate([vt_ref[0, j, h * hd:(h + 1) * hd, :], ones_rows], axis=0)
            acc_scr[h] = tile_rows(alpha, ACC_ROWS) * acc_scr[h] + _dot(vt_ones, p)

    def attend(chunks):
        staged = [logits_phase(j, kind) for j, kind in chunks]
        probs = [exp_phase(st) for st in staged]
        for (j, _), pr in zip(chunks, probs):
            output_phase(j, pr)

    n_far = jnp.maximum(qi - 1, 0)

    def far_group(i, carry):
        attend([(FAR_GROUP * i + c, "far") for c in range(FAR_GROUP)])
        return carry

    lax.fori_loop(0, n_far // FAR_GROUP, far_group, 0)

    def far_single(j, carry):
        attend([(j, "far")])
        return carry

    lax.fori_loop(n_far - n_far % FAR_GROUP, n_far, far_single, 0)

    @pl.when(qi >= 1)
    def _():
        attend([(qi - 1, "prev"), (qi, "diag")])

    @pl.when(qi == 0)
    def _():
        attend([(qi, "diag")])

    out_t = jnp.concatenate(
        [acc_scr[h, 0:hd, :] / acc_scr[h, hd:hd + 1, :] for h in range(N_HEADS)], axis=0)
    o_ref[...] = out_t.T.astype(o_ref.dtype)


def _attention(zb, vt, zc, kmean, tdiag, tprev, batch, seq):
    blk = ATT_BLOCK
    nq = seq // blk
    nb = kmean.shape[1]
    topk = min(DSA_TOPK_MAX, seq // 4)
    kernel = functools.partial(_attn_kernel, topk=topk)
    resident = dict(pipeline_mode=pl.Buffered(1))
    return pl.pallas_call(
        kernel,
        grid=(batch, nq),
        in_specs=[
            pl.BlockSpec((blk, W8), lambda b, i: (b * nq + i, 0)),
            pl.BlockSpec((seq, W8), lambda b, i: (b, 1)),
            pl.BlockSpec((1, nq, W8, blk), lambda b, i: (b, 0, 0, 0)),
            pl.BlockSpec((blk, W8), lambda b, i: (b * nq + i, 2)),
            pl.BlockSpec((seq, LANES), lambda b, i: (b, 3 * W8 // LANES)),
            pl.BlockSpec((blk, LANES), lambda b, i: (b * nq + i, 0)),
            pl.BlockSpec((1, nb, GROUP_WIDTH), lambda b, i: (b, 0, 0)),
            pl.BlockSpec((N_HEADS, blk, blk), lambda b, i: (0, 0, 0), **resident),
            pl.BlockSpec((N_HEADS, blk, blk), lambda b, i: (0, 0, 0), **resident),
        ],
        out_specs=pl.BlockSpec((blk, W8), lambda b, i: (b * nq + i, 0)),
        out_shape=jax.ShapeDtypeStruct((batch * seq, W8), jnp.bfloat16),
        scratch_shapes=[
            pltpu.VMEM((nq, blk, blk), jnp.int32),
            pltpu.VMEM((nq, blk, blk), jnp.int16),
            pltpu.VMEM((nq, blk, blk), jnp.int16),
            pltpu.VMEM((blk, blk), jnp.float32),
            pltpu.VMEM((nq, blk, blk), jnp.float32),
            pltpu.VMEM((4, nb, blk), jnp.float32),
            pltpu.VMEM((N_HEADS, blk, LANES), jnp.bfloat16),
            pltpu.VMEM((N_HEADS, SUBLANES, blk), jnp.float32),
            pltpu.VMEM((N_HEADS, ACC_ROWS, blk), jnp.float32),
        ],
        compiler_params=pltpu.CompilerParams(
            dimension_semantics=("arbitrary", "arbitrary"), vmem_limit_bytes=VMEM_LIMIT),
    )(zb, zb, vt, zb, zb, zc, kmean, tdiag, tprev)


def _tail_kernel(x_ref, yab_ref, ycd_ref, p_ref, wo_ref, gpost_ref, gfpre_ref, wgu_ref, wd_ref,
                 gfpost_ref, gple_ref, wpg_ref, wpp_ref, o_ref, act_scr):
    tm = x_ref.shape[0]
    half = yab_ref.shape[1]
    d_ff = wd_ref.shape[0]
    subs = [slice(r * TAIL_SUB, (r + 1) * TAIL_SUB) for r in range(tm // TAIL_SUB)]

    mix = [_dot(yab_ref[r, :], wo_ref[0:half, :]) + _dot(ycd_ref[r, :], wo_ref[half:2 * half, :])
           for r in subs]
    x1 = [x_ref[r, :] + _rms(m, gpost_ref[...]) for r, m in zip(subs, mix)]
    h2 = [_rms(x, gfpre_ref[...]).astype(jnp.bfloat16) for x in x1]
    for c0 in range(0, d_ff, FFN_CHUNK):
        cw = min(FFN_CHUNK, d_ff - c0)
        gates = [_dot(h, wgu_ref[:, c0:c0 + cw]) for h in h2]
        ups = [_dot(h, wgu_ref[:, d_ff + c0:d_ff + c0 + cw]) for h in h2]
        for r, g, u in zip(subs, gates, ups):
            act_scr[r, c0:c0 + cw] = (g * jax.nn.sigmoid(g) * u).astype(jnp.bfloat16)
    f = [_dot(act_scr[r, :], wd_ref[...]) for r in subs]
    x2 = [x + _rms(y, gfpost_ref[...]) for x, y in zip(x1, f)]
    hg = [_rms(x, gple_ref[...]).astype(jnp.bfloat16) for x in x2]
    gate = [jax.nn.sigmoid(_dot(h, wpg_ref[...])) for h in hg]
    proj = [_dot(p_ref[r, :].astype(jnp.bfloat16), wpp_ref[...]) for r in subs]
    for r, x, g, pr in zip(subs, x2, gate, proj):
        o_ref[r, :] = x + g * pr


def _tail(x2d, yab, ycd, p2d, wo, gpost, gfpre, wgu, wd, gfpost, gple, wpg, wpp):
    n, d = x2d.shape
    tm = ROW_TILE
    row = lambda i: (i, 0)
    const2 = lambda i: (0, 0)
    resident = dict(pipeline_mode=pl.Buffered(1))
    vec = pl.BlockSpec((1, d), const2)
    return pl.pallas_call(
        _tail_kernel,
        grid=(n // tm,),
        in_specs=[
            pl.BlockSpec((tm, d), row),
            pl.BlockSpec((tm, yab.shape[1]), row),
            pl.BlockSpec((tm, ycd.shape[1]), row),
            pl.BlockSpec((tm, p2d.shape[1]), row),
            pl.BlockSpec(wo.shape, const2, **resident),
            vec, vec,
            pl.BlockSpec(wgu.shape, const2, **resident),
            pl.BlockSpec(wd.shape, const2, **resident),
            vec, vec,
            pl.BlockSpec(wpg.shape, const2, **resident),
            pl.BlockSpec(wpp.shape, const2, **resident),
        ],
        out_specs=pl.BlockSpec((tm, d), row),
        out_shape=jax.ShapeDtypeStruct((n, d), jnp.float32),
        scratch_shapes=[pltpu.VMEM((tm, wd.shape[0]), jnp.bfloat16)],
        compiler_params=pltpu.CompilerParams(
            dimension_semantics=("arbitrary",), vmem_limit_bytes=VMEM_LIMIT),
    )(x2d, yab, ycd, p2d, wo, gpost, gfpre, wgu, wd, gfpost, gple, wpg, wpp)


def _rel_bucket_np(dist):
    n = np.maximum(dist, 0)
    max_exact = REL_BUCKETS // 2
    nf = np.maximum(n, 1).astype(np.float32)
    large = max_exact + (np.log(nf / np.float32(max_exact)) / np.float32(math.log(REL_MAX_DIST / max_exact))
                         * np.float32(REL_BUCKETS - max_exact)).astype(np.int32)
    large = np.minimum(large, REL_BUCKETS - 1)
    return np.where(n < max_exact, n, large)


def _bias_tables(rel_bias, seq):
    blk = ATT_BLOCK
    key = np.arange(blk)[:, None]
    qry = np.arange(blk)[None, :]
    far = _rel_bucket_np(np.arange(blk + 1, max(seq, blk + 2)))
    assert (far == far[0]).all(), "bias must be constant beyond the previous chunk"
    tab = rel_bias.astype(jnp.float32)
    tab = (tab - tab[:, int(far[0])][:, None]) * LOG2E

    def table(bucket):
        onehot = (jnp.asarray(bucket)[None] == jnp.arange(REL_BUCKETS)[:, None, None]).astype(jnp.float32)
        return jnp.einsum("hb,bkq->hkq", tab, onehot, precision=lax.Precision.HIGHEST)

    return table(_rel_bucket_np(qry - key)), table(_rel_bucket_np(qry - key + blk))


def _regroup_w_in(w_in):
    gw = GROUP_WIDTH
    sizes = (gw, gw, gw, gw, gw, gw, gw, IDX_HEADS * IDX_DIM, IDX_DIM, IDX_HEADS, gw, gw, gw)
    offs = np.concatenate([[0], np.cumsum(sizes)])
    a_in, a_c, a_b, pv, cq, ck, cv, iq, ik, iw, dq, dk, dv = (
        w_in[..., offs[i]:offs[i + 1]] for i in range(len(sizes)))
    zeros = lambda w: jnp.zeros(w_in.shape[:-1] + (w,), w_in.dtype)
    qs = LOG2E * HEAD_DIM ** -0.5
    cols = [a_in, a_c, a_b, pv,
            cq * qs, dq * qs, ck, dk, iq, ik, ik,
            cv, dv,
            iw, zeros(LANES - IDX_HEADS)]
    return jnp.concatenate(cols, axis=-1).astype(jnp.bfloat16)


def _block_diag(pool_w):
    depth, ng, g, _ = pool_w.shape
    eye = jnp.eye(ng, dtype=pool_w.dtype)
    bd = jnp.einsum("lgcd,gh->lgchd", pool_w, eye).reshape(depth, ng * g, ng * g)
    return bd.astype(jnp.bfloat16)


@jax.jit
def kernel(x, p, rel_bias, g_mix_pre, w_in, conv_w, pool_w, pool_scale, w_out, g_mix_post, g_ffn_pre, w_gate_up, w_down, g_ffn_post, g_ple, w_ple_gate, w_ple_proj):
    batch, seq, d = x.shape
    depth = w_in.shape[0]
    n = batch * seq
    d_ff = w_down.shape[1]
    assert seq % ROW_TILE == 0 and d_ff % COL_CHUNK == 0
    bf16 = jnp.bfloat16

    tdiag, tprev = _bias_tables(rel_bias, seq)
    w_in_r = _regroup_w_in(w_in)
    pool_bd = _block_diag(pool_w)
    wgu = w_gate_up.astype(bf16)
    wd = w_down.astype(bf16)
    wo = w_out.astype(bf16)
    wpg = w_ple_gate.astype(bf16)
    wpp = w_ple_proj.astype(bf16)

    x2d = x.reshape(n, d)
    for i in range(depth):
        yab, zb, vt, zc, kmean = _inproj(x2d, g_mix_pre[i][None, :], w_in_r[i], conv_w[i], pool_bd[i],
                                         pool_scale[i][None, :], batch, seq)
        ycd = _attention(zb, vt, zc, kmean.reshape(batch, seq // MOBA_BLOCK, GROUP_WIDTH),
                         tdiag, tprev, batch, seq)
        x2d = _tail(x2d, yab, ycd, p[i].reshape(n, -1), wo[i], g_mix_post[i][None, :],
                    g_ffn_pre[i][None, :], wgu[i], wd[i], g_ffn_post[i][None, :],
                    g_ple[i][None, :], wpg[i], wpp[i])
    return x2d.reshape(batch, seq, d)
```

```python
import functools
import math

import numpy as np
import jax
import jax.numpy as jnp
from jax import lax
from jax.experimental import pallas as pl
from jax.experimental.pallas import tpu as pltpu

HEAD_DIM = 64
GROUP_WIDTH = 256
CONV_WIDTH = 3
POOL_WINDOWS = (2, 4, 8, 16)
POOL_GROUP = GROUP_WIDTH // len(POOL_WINDOWS)
IDX_HEADS = 8
IDX_DIM = 64
DSA_TOPK_MAX = 256
MOBA_BLOCK = 256
MOBA_TOPB_MAX = 3
REL_BUCKETS = 32
REL_MAX_DIST = 128
N_HEADS = 8
RMS_EPS = 1e-6

LANES = 128
SUBLANES = 8
ATT_BLOCK = 256
FAR_GROUP = 3
SCORE_GROUP = 3
HALO = 16
NEG = -1e30
LOG2E = math.log2(math.e)
ACC_ROWS = HEAD_DIM + 16
INT_MIN = -(2 ** 31)
VMEM_LIMIT = 56 * 1024 * 1024
ROW_TILE = 512
TAIL_SUB = 256
FFN_CHUNK = 1024

gw_ = GROUP_WIDTH
SRC = dict(a_in=0, a_c=gw_, a_b=2 * gw_, pool_v=3 * gw_, cq=4 * gw_, ck=5 * gw_, cv=6 * gw_, iq=7 * gw_)
MAIN_COLS = 7 * gw_ + IDX_HEADS * IDX_DIM
TAIL = dict(dq=0, dk=gw_, dv=2 * gw_, ik2=3 * gw_, iw=3 * gw_ + LANES)
TAIL_COLS = 3 * gw_ + 2 * LANES
W8 = N_HEADS * HEAD_DIM
ZB_COLS = 2 * W8 + IDX_HEADS * IDX_DIM + 2 * IDX_DIM
ZV_COLS = W8
ZC_COLS = LANES
COL_CHUNK = 256
QSCALE = LOG2E * HEAD_DIM ** -0.5


def _rms(x, g):
    return x * lax.rsqrt(jnp.mean(x * x, axis=-1, keepdims=True) + RMS_EPS) * g


def _dot(a, b):
    return jnp.dot(a, b, preferred_element_type=jnp.float32)


def _dot_t(a, b):
    return lax.dot_general(a, b, (((1,), (1,)), ((), ())), preferred_element_type=jnp.float32)


def _inproj_kernel(x_ref, g_ref, w_ref, wt_ref, cw_ref, pw_ref, ps_ref, yab_ref, zb_ref, vt_ref,
                   zc_ref, kmean_ref, hbuf, vbuf, *, tiles_per_seq):
    tm = x_ref.shape[0]
    blk = ATT_BLOCK
    gw = GROUP_WIDTH
    h = _rms(x_ref[...], g_ref[...]).astype(jnp.bfloat16)

    def main(name, off=0, width=gw):
        c0 = SRC[name] + off
        return _dot(h, w_ref[:, c0:c0 + width])

    def tail(name, width=gw):
        return _dot(h, wt_ref[:, TAIL[name]:TAIL[name] + width])

    mixer_in = [main("a_in"), main("a_c"), main("a_b"), main("pool_v")]
    zb_ref[:, 0:gw] = (main("cq") * QSCALE).astype(jnp.bfloat16)
    zb_ref[:, gw:2 * gw] = (tail("dq") * QSCALE).astype(jnp.bfloat16)
    zb_ref[:, 2 * gw:3 * gw] = main("ck").astype(jnp.bfloat16)
    dk = tail("dk")
    zb_ref[:, 3 * gw:4 * gw] = dk.astype(jnp.bfloat16)
    for r in range(tm // MOBA_BLOCK):
        kmean_ref[r] = jnp.mean(dk[r * MOBA_BLOCK:(r + 1) * MOBA_BLOCK], axis=0, keepdims=True)
    for c0 in range(0, IDX_HEADS * IDX_DIM, gw):
        zb_ref[:, 2 * W8 + c0:2 * W8 + c0 + gw] = main("iq", c0).astype(jnp.bfloat16)
    zb_ref[:, ZB_COLS - LANES:ZB_COLS] = tail("ik2", LANES).astype(jnp.bfloat16)
    for c0, z in ((0, main("cv")), (gw, tail("dv"))):
        for r in range(tm // blk):
            vt_ref[0, r, c0:c0 + gw, :] = z[r * blk:(r + 1) * blk, :].T.astype(jnp.bfloat16)
    zc_ref[...] = tail("iw", LANES)
    _conv_pool(*mixer_in, pl.program_id(0) % tiles_per_seq, cw_ref, pw_ref, ps_ref, yab_ref, hbuf, vbuf)


def _inproj(x2d, g, w, wt, conv_w, pool_w_bd, pool_scale, batch, seq):
    n, d = x2d.shape
    tm = ROW_TILE
    blk = ATT_BLOCK
    tiles_per_seq = seq // tm
    const = lambda i: (0, 0)
    return pl.pallas_call(
        functools.partial(_inproj_kernel, tiles_per_seq=tiles_per_seq),
        grid=(n // tm,),
        in_specs=[
            pl.BlockSpec((tm, d), lambda i: (i, 0)),
            pl.BlockSpec((1, d), const),
            pl.BlockSpec((d, MAIN_COLS), const, pipeline_mode=pl.Buffered(1)),
            pl.BlockSpec((d, TAIL_COLS), const, pipeline_mode=pl.Buffered(1)),
            pl.BlockSpec((CONV_WIDTH, GROUP_WIDTH), const),
            pl.BlockSpec((GROUP_WIDTH, GROUP_WIDTH), const),
            pl.BlockSpec((1, GROUP_WIDTH), const),
        ],
        out_specs=[
            pl.BlockSpec((tm, 2 * GROUP_WIDTH), lambda i: (i, 0)),
            pl.BlockSpec((tm, ZB_COLS), lambda i: (i, 0)),
            pl.BlockSpec((1, tm // blk, ZV_COLS, blk),
                         lambda i: (i // tiles_per_seq, i % tiles_per_seq, 0, 0)),
            pl.BlockSpec((tm, ZC_COLS), lambda i: (i, 0)),
            pl.BlockSpec((tm // MOBA_BLOCK, 1, GROUP_WIDTH), lambda i: (i, 0, 0)),
        ],
        out_shape=[
            jax.ShapeDtypeStruct((n, 2 * GROUP_WIDTH), jnp.bfloat16),
            jax.ShapeDtypeStruct((n, ZB_COLS), jnp.bfloat16),
            jax.ShapeDtypeStruct((batch, seq // blk, ZV_COLS, blk), jnp.bfloat16),
            jax.ShapeDtypeStruct((n, ZC_COLS), jnp.float32),
            jax.ShapeDtypeStruct((n // MOBA_BLOCK, 1, GROUP_WIDTH), jnp.float32),
        ],
        scratch_shapes=[pltpu.VMEM((HALO + tm, GROUP_WIDTH), jnp.float32),
                        pltpu.VMEM((HALO + tm, GROUP_WIDTH), jnp.float32)],
        compiler_params=pltpu.CompilerParams(
            dimension_semantics=("arbitrary",), vmem_limit_bytes=VMEM_LIMIT),
    )(x2d, g, w, wt, conv_w, pool_w_bd, pool_scale)


def _conv_pool(a_in, a_c, a_b, v, i, cw_ref, pw_ref, ps_ref, y_ref, hbuf, vbuf):
    ts = a_in.shape[0]
    gw = GROUP_WIDTH

    @pl.when(i == 0)
    def _():
        hbuf[0:HALO, :] = jnp.zeros((HALO, gw), jnp.float32)
        vbuf[0:HALO, :] = jnp.zeros((HALO, gw), jnp.float32)

    @pl.when(i > 0)
    def _():
        hbuf[0:HALO, :] = hbuf[ts:ts + HALO, :]
        vbuf[0:HALO, :] = vbuf[ts:ts + HALO, :]

    hbuf[HALO:HALO + ts, :] = a_c * a_in
    vbuf[HALO:HALO + ts, :] = v

    def hist(buf, d, lo, hi):
        return buf[HALO - d:HALO - d + ts, lo:hi]

    conv = (cw_ref[0:1, :] * hist(hbuf, 2, 0, gw) + cw_ref[1:2, :] * hist(hbuf, 1, 0, gw)
            + cw_ref[2:3, :] * hist(hbuf, 0, 0, gw))
    y_ref[:, 0:gw] = (a_b * conv).astype(y_ref.dtype)

    t_pos = i * ts + lax.broadcasted_iota(jnp.int32, (ts, LANES), 0)
    lane = lax.broadcasted_iota(jnp.int32, (ts, LANES), 1)
    first_group = lane < POOL_GROUP
    halves = []
    for half, (w_small, w_big) in enumerate(((2, 4), (8, 16))):
        lo, hi = half * LANES, (half + 1) * LANES
        acc = hist(vbuf, 0, lo, hi)
        for d in range(1, w_small):
            acc = acc + hist(vbuf, d, lo, hi)
        s_small = acc
        for d in range(w_small, w_big):
            acc = acc + hist(vbuf, d, lo, hi)
        wsum = jnp.where(first_group, s_small, acc)
        cnt = jnp.minimum(t_pos + 1, jnp.where(first_group, w_small, w_big)).astype(jnp.float32)
        halves.append(wsum / cnt - hist(vbuf, 0, lo, hi))
    dmat = jnp.concatenate(halves, axis=1).astype(jnp.bfloat16)
    yb = _dot(dmat, pw_ref[...]) * ps_ref[...]
    y_ref[:, gw:2 * gw] = yb.astype(y_ref.dtype)


def _attn_kernel(q_ref, k_ref, vt_ref, iq_ref, ik_ref, iw_ref, kmean_ref, tdiag_ref, tprev_ref,
                 o_ref, keys_scr, hi_scr, lo_scr, gmax_scr, mb_scr, moba_scr, qpad_scr, m_scr,
                 acc_scr, *, topk):
    qi = pl.program_id(1)
    blk = ATT_BLOCK
    hd = HEAD_DIM
    n_chunks = qi + 1
    key_idx = lax.broadcasted_iota(jnp.int32, (blk, blk), 0)
    qry_idx = lax.broadcasted_iota(jnp.int32, (blk, blk), 1)
    causal = key_idx <= qry_idx

    def fold(x):
        return jnp.sum(x.reshape(blk // SUBLANES, SUBLANES, blk), axis=0)

    iq = iq_ref[...]
    iw_t = iw_ref[...].T
    idx_scale = (IDX_HEADS ** -0.5) * (IDX_DIM ** -0.5)

    lane_q = lax.broadcasted_iota(jnp.int32, (blk, LANES), 1)

    def score_keys(j):
        ik2 = ik_ref[pl.ds(pl.multiple_of(j * blk, blk), blk), :]
        ik_half = [jnp.where(lane_q < IDX_DIM, ik2, jnp.zeros_like(ik2)),
                   jnp.where(lane_q >= IDX_DIM, ik2, jnp.zeros_like(ik2))]
        acc = jnp.zeros((blk, blk), jnp.float32)
        for h in range(IDX_HEADS):
            s = _dot_t(ik_half[h % 2], iq[:, (h // 2) * LANES:(h // 2 + 1) * LANES])
            acc = acc + jnp.maximum(s, 0.0) * iw_t[h:h + 1, :]
        return acc * idx_scale + 0.0

    def to_key(score):
        bits = pltpu.bitcast(score, jnp.int32)
        return jnp.where(bits < 0, bits ^ jnp.int32(0x7FFFFFFF), bits)

    def hi16(key):
        return lax.shift_right_arithmetic(key, 16).astype(jnp.int16)

    def lo16(key):
        return ((key & 0xFFFF) - 0x8000).astype(jnp.int16)

    def store_keys(j, key):
        keys_scr[j] = key
        hi_scr[j] = hi16(key)
        lo_scr[j] = lo16(key)

    def score_body(j, carry):
        score = score_keys(j)
        store_keys(j, to_key(score))
        gmax_scr[...] = jnp.maximum(gmax_scr[...], score)
        return carry

    gmax_scr[...] = jnp.full((blk, blk), -jnp.inf, jnp.float32)

    def score_group(i, carry):
        js = [SCORE_GROUP * i + c for c in range(SCORE_GROUP)]
        scores = [score_keys(j) for j in js]
        best = gmax_scr[...]
        for j, sc in zip(js, scores):
            store_keys(j, to_key(sc))
            best = jnp.maximum(best, sc)
        gmax_scr[...] = best
        return carry

    lax.fori_loop(0, qi // SCORE_GROUP, score_group, 0)
    lax.fori_loop(qi - qi % SCORE_GROUP, qi, score_body, 0)
    score = jnp.where(causal, score_keys(qi), -jnp.inf)
    store_keys(qi, jnp.where(causal, to_key(score), jnp.int32(INT_MIN)))
    gmax = jnp.maximum(gmax_scr[...], score)

    assert blk >= topk
    slot_min = jnp.min(gmax, axis=0, keepdims=True)
    lo_u = jnp.where(slot_min == -jnp.inf, jnp.int32(INT_MIN), to_key(slot_min)) ^ jnp.int32(INT_MIN)
    hi_u = to_key(jnp.max(gmax, axis=0, keepdims=True)) ^ jnp.int32(INT_MIN)
    open_bits = 32 - lax.clz(lo_u ^ hi_u)
    n_bits = jnp.max(open_bits.astype(jnp.float32)).astype(jnp.int32)
    low_mask = jnp.where(n_bits == 0, jnp.int32(0),
                         lax.shift_right_logical(jnp.int32(-1), (32 - n_bits) & 31))

    def count_keys(pred):
        def body(j, c):
            return c + fold(jnp.where(pred(keys_scr[j]), 1, 0))
        c = lax.fori_loop(0, n_chunks, body, jnp.zeros((SUBLANES, blk), jnp.int32))
        return jnp.sum(c, axis=0, keepdims=True)

    half_rows = 2 * SUBLANES

    def count_halves(ref, pred):
        def body(j, c):
            ge = jnp.where(pred(ref[j]), jnp.int16(1), jnp.int16(0))
            parts = [ge[r * half_rows:(r + 1) * half_rows] for r in range(blk // half_rows)]
            while len(parts) > 1:
                parts = [a + b for a, b in zip(parts[::2], parts[1::2])]
            return c + parts[0]
        c = lax.fori_loop(0, n_chunks, body, jnp.zeros((half_rows, blk), jnp.int16))
        return jnp.sum(c.astype(jnp.int32), axis=0, keepdims=True)

    def search_bits(first_it, last_it, count_ge, carry):
        def bit_body(it, carry):
            t_u, cnt_ge = carry
            cand_u = t_u | lax.shift_left(jnp.int32(1), 31 - it)
            cnt = count_ge(cand_u ^ jnp.int32(INT_MIN))
            take = cnt >= topk
            return jnp.where(take, cand_u, t_u), jnp.where(take, cnt, cnt_ge)
        return lax.fori_loop(first_it, last_it, bit_body, carry)

    t_u0 = lo_u & ~low_mask
    cand0 = t_u0 ^ jnp.int32(INT_MIN)
    carry = (t_u0, count_keys(lambda kj: kj >= cand0))
    first_it = 32 - n_bits
    carry = search_bits(first_it, 16, lambda cand: count_halves(hi_scr, lambda v: v >= hi16(cand)), carry)
    thr_hi = hi16(carry[0] ^ jnp.int32(INT_MIN))
    above = count_halves(hi_scr, lambda v: v > thr_hi)

    def open_body(j, c):
        lo_scr[j] = jnp.where(hi_scr[j] == thr_hi, lo_scr[j], jnp.int16(-0x8000))
        return c
    lax.fori_loop(0, n_chunks, open_body, 0)
    t_u, cnt_ge = search_bits(
        jnp.maximum(first_it, 16), 32,
        lambda cand: above + count_halves(lo_scr, lambda v: v >= lo16(cand)), carry)
    thr = t_u ^ jnp.int32(INT_MIN)
    tie_any = jnp.max(cnt_ge) > topk

    @pl.when(jnp.logical_not(tie_any))
    def _():
        def body(j, carry):
            mb_scr[j] = jnp.where(keys_scr[j] >= thr, 0.0, NEG)
            return carry
        lax.fori_loop(0, qi, body, 0)
        mb_scr[qi] = jnp.where((keys_scr[qi] >= thr) & causal, 0.0, NEG)

    @pl.when(tie_any)
    def _():
        need = (topk - count_keys(lambda kj: kj > thr)).astype(jnp.float32)
        lower = jnp.where(qry_idx <= key_idx, 1.0, 0.0).astype(jnp.bfloat16)

        def sel_chunk(j, base):
            kj = keys_scr[j]
            eq = kj == thr
            pref = _dot(lower, jnp.where(eq, 1.0, 0.0).astype(jnp.bfloat16)) + base
            return (kj > thr) | (eq & (pref <= need)), pref[blk - 1:blk, :]

        def body(j, base):
            sel, base = sel_chunk(j, base)
            mb_scr[j] = jnp.where(sel, 0.0, NEG)
            return base
        base = lax.fori_loop(0, qi, body, jnp.zeros((1, blk), jnp.float32))
        sel, _ = sel_chunk(qi, base)
        mb_scr[qi] = jnp.where(sel & causal, 0.0, NEG)

    q8 = q_ref[...]
    nb = kmean_ref.shape[1]
    blk_idx = lax.broadcasted_iota(jnp.int32, (nb, blk), 0)
    blk_idx_f = blk_idx.astype(jnp.float32)
    km = kmean_ref[0]
    km_hi = km.astype(jnp.bfloat16)
    km_lo = (km - km_hi.astype(jnp.float32)).astype(jnp.bfloat16)
    for h in range(4):
        qh = q8[:, (4 + h) * hd:(5 + h) * hd]
        gate = _dot_t(km_hi[:, h * hd:(h + 1) * hd], qh) + _dot_t(km_lo[:, h * hd:(h + 1) * hd], qh)
        gate = jnp.where(blk_idx < qi, gate, -jnp.inf)
        chosen = jnp.zeros((nb, blk), jnp.bool_)
        for _ in range(MOBA_TOPB_MAX):
            mx = jnp.max(gate, axis=0, keepdims=True)
            is_mx = (gate == mx) & (gate > -jnp.inf)
            first = jnp.min(jnp.where(is_mx, blk_idx_f, float(nb)), axis=0, keepdims=True)
            pick = blk_idx_f == first
            chosen = chosen | pick
            gate = jnp.where(pick, -jnp.inf, gate)
        moba_scr[h] = jnp.where(chosen, 0.0, NEG)

    m_scr[...] = jnp.full(m_scr.shape, NEG, jnp.float32)
    acc_scr[...] = jnp.zeros(acc_scr.shape, jnp.float32)
    ones_rows = jnp.ones((ACC_ROWS - hd, blk), jnp.bfloat16)

    for h in range(N_HEADS):
        pair = q8[:, (h // 2) * LANES:(h // 2 + 1) * LANES]
        qpad_scr[h] = jnp.where((lane_q >= hd) == (h % 2 == 1), pair, jnp.zeros_like(pair))

    def tile_rows(x, rows):
        return jnp.broadcast_to(x[None], (rows // SUBLANES, SUBLANES, blk)).reshape(rows, blk)

    def logits_phase(j, kind):
        kc = k_ref[pl.ds(pl.multiple_of(j * blk, blk), blk), :]
        staged = []
        for h in range(N_HEADS):
            s = _dot_t(kc[:, (h // 2) * LANES:(h // 2 + 1) * LANES], qpad_scr[h])
            if kind == "prev":
                s = s + tprev_ref[h]
            elif kind == "diag":
                s = s + tdiag_ref[h]
            if h < 4:
                s = s + mb_scr[j]
            elif kind == "diag":
                s = jnp.where(causal, s, NEG)
            mx = jnp.max(jnp.max(s.reshape(blk // SUBLANES, SUBLANES, blk), axis=0), axis=0, keepdims=True)
            m_prev = m_scr[h]
            m_new = jnp.maximum(m_prev, mx)
            shift = m_new
            if h >= 4 and kind != "diag":
                taken = moba_scr[h - 4, pl.ds(j, 1), :] == 0.0
                m_new = jnp.where(taken, m_new, m_prev)
                shift = jnp.where(taken, m_new, -NEG)
            m_scr[h] = m_new
            staged.append((s, shift, jnp.exp2(m_prev - m_new)))
        return staged

    def exp_phase(staged):
        return [(jnp.exp2(s - tile_rows(shift, blk)).astype(jnp.bfloat16), alpha)
                for s, shift, alpha in staged]

    def output_phase(j, probs):
        for h, (p, alpha) in enumerate(probs):
            vt_ones = jnp.concatenate([vt_ref[0, j, h * hd:(h + 1) * hd, :], ones_rows], axis=0)
            acc_scr[h] = tile_rows(alpha, ACC_ROWS) * acc_scr[h] + _dot(vt_ones, p)

    def attend(chunks):
        staged = [logits_phase(j, kind) for j, kind in chunks]
        probs = [exp_phase(st) for st in staged]
        for (j, _), pr in zip(chunks, probs):
            output_phase(j, pr)

    n_far = jnp.maximum(qi - 1, 0)

    def far_group(i, carry):
        attend([(FAR_GROUP * i + c, "far") for c in range(FAR_GROUP)])
        return carry

    lax.fori_loop(0, n_far // FAR_GROUP, far_group, 0)

    def far_single(j, carry):
        attend([(j, "far")])
        return carry

    lax.fori_loop(n_far - n_far % FAR_GROUP, n_far, far_single, 0)

    @pl.when(qi >= 1)
    def _():
        attend([(qi - 1, "prev"), (qi, "diag")])

    @pl.when(qi == 0)
    def _():
        attend([(qi, "diag")])

    out_t = jnp.concatenate(
        [acc_scr[h, 0:hd, :] / acc_scr[h, hd:hd + 1, :] for h in range(N_HEADS)], axis=0)
    o_ref[...] = out_t.T.astype(o_ref.dtype)


def _attention(zb, vt, zc, kmean, tdiag, tprev, batch, seq):
    blk = ATT_BLOCK
    nq = seq // blk
    nb = kmean.shape[1]
    topk = min(DSA_TOPK_MAX, seq // 4)
    kernel = functools.partial(_attn_kernel, topk=topk)
    resident = dict(pipeline_mode=pl.Buffered(1))
    return pl.pallas_call(
        kernel,
        grid=(batch, nq),
        in_specs=[
            pl.BlockSpec((blk, W8), lambda b, i: (b * nq + i, 0)),
            pl.BlockSpec((seq, W8), lambda b, i: (b, 1)),
            pl.BlockSpec((1, nq, W8, blk), lambda b, i: (b, 0, 0, 0)),
            pl.BlockSpec((blk, W8), lambda b, i: (b * nq + i, 2)),
            pl.BlockSpec((seq, LANES), lambda b, i: (b, 3 * W8 // LANES)),
            pl.BlockSpec((blk, LANES), lambda b, i: (b * nq + i, 0)),
            pl.BlockSpec((1, nb, GROUP_WIDTH), lambda b, i: (b, 0, 0)),
            pl.BlockSpec((N_HEADS, blk, blk), lambda b, i: (0, 0, 0), **resident),
            pl.BlockSpec((N_HEADS, blk, blk), lambda b, i: (0, 0, 0), **resident),
        ],
        out_specs=pl.BlockSpec((blk, W8), lambda b, i: (b * nq + i, 0)),
        out_shape=jax.ShapeDtypeStruct((batch * seq, W8), jnp.bfloat16),
        scratch_shapes=[
            pltpu.VMEM((nq, blk, blk), jnp.int32),
            pltpu.VMEM((nq, blk, blk), jnp.int16),
            pltpu.VMEM((nq, blk, blk), jnp.int16),
            pltpu.VMEM((blk, blk), jnp.float32),
            pltpu.VMEM((nq, blk, blk), jnp.float32),
            pltpu.VMEM((4, nb, blk), jnp.float32),
            pltpu.VMEM((N_HEADS, blk, LANES), jnp.bfloat16),
            pltpu.VMEM((N_HEADS, SUBLANES, blk), jnp.float32),
            pltpu.VMEM((N_HEADS, ACC_ROWS, blk), jnp.float32),
        ],
        compiler_params=pltpu.CompilerParams(
            dimension_semantics=("arbitrary", "arbitrary"), vmem_limit_bytes=VMEM_LIMIT),
    )(zb, zb, vt, zb, zb, zc, kmean, tdiag, tprev)


def _tail_kernel(x_ref, yab_ref, ycd_ref, p_ref, wo_ref, gpost_ref, gfpre_ref, wgu_ref, wd_ref,
                 gfpost_ref, gple_ref, wpg_ref, wpp_ref, o_ref, act_scr):
    tm = x_ref.shape[0]
    half = yab_ref.shape[1]
    d_ff = wd_ref.shape[0]
    subs = [slice(r * TAIL_SUB, (r + 1) * TAIL_SUB) for r in range(tm // TAIL_SUB)]

    mix = [_dot(yab_ref[r, :], wo_ref[0:half, :]) + _dot(ycd_ref[r, :], wo_ref[half:2 * half, :])
           for r in subs]
    x1 = [x_ref[r, :] + _rms(m, gpost_ref[...]) for r, m in zip(subs, mix)]
    h2 = [_rms(x, gfpre_ref[...]).astype(jnp.bfloat16) for x in x1]
    for c0 in range(0, d_ff, FFN_CHUNK):
        cw = min(FFN_CHUNK, d_ff - c0)
        gates = [_dot(h, wgu_ref[:, c0:c0 + cw]) for h in h2]
        ups = [_dot(h, wgu_ref[:, d_ff + c0:d_ff + c0 + cw]) for h in h2]
        for r, g, u in zip(subs, gates, ups):
            act_scr[r, c0:c0 + cw] = (g * jax.nn.sigmoid(g) * u).astype(jnp.bfloat16)
    f = [_dot(act_scr[r, :], wd_ref[...]) for r in subs]
    x2 = [x + _rms(y, gfpost_ref[...]) for x, y in zip(x1, f)]
    hg = [_rms(x, gple_ref[...]).astype(jnp.bfloat16) for x in x2]
    gate = [jax.nn.sigmoid(_dot(h, wpg_ref[...])) for h in hg]
    proj = [_dot(p_ref[r, :].astype(jnp.bfloat16), wpp_ref[...]) for r in subs]
    for r, x, g, pr in zip(subs, x2, gate, proj):
        o_ref[r, :] = x + g * pr


def _tail(x2d, yab, ycd, p2d, wo, gpost, gfpre, wgu, wd, gfpost, gple, wpg, wpp):
    n, d = x2d.shape
    tm = ROW_TILE
    row = lambda i: (i, 0)
    const2 = lambda i: (0, 0)
    resident = dict(pipeline_mode=pl.Buffered(1))
    vec = pl.BlockSpec((1, d), const2)
    return pl.pallas_call(
        _tail_kernel,
        grid=(n // tm,),
        in_specs=[
            pl.BlockSpec((tm, d), row),
            pl.BlockSpec((tm, yab.shape[1]), row),
            pl.BlockSpec((tm, ycd.shape[1]), row),
            pl.BlockSpec((tm, p2d.shape[1]), row),
            pl.BlockSpec(wo.shape, const2, **resident),
            vec, vec,
            pl.BlockSpec(wgu.shape, const2, **resident),
            pl.BlockSpec(wd.shape, const2, **resident),
            vec, vec,
            pl.BlockSpec(wpg.shape, const2, **resident),
            pl.BlockSpec(wpp.shape, const2, **resident),
        ],
        out_specs=pl.BlockSpec((tm, d), row),
        out_shape=jax.ShapeDtypeStruct((n, d), jnp.float32),
        scratch_shapes=[pltpu.VMEM((tm, wd.shape[0]), jnp.bfloat16)],
        compiler_params=pltpu.CompilerParams(
            dimension_semantics=("arbitrary",), vmem_limit_bytes=VMEM_LIMIT),
    )(x2d, yab, ycd, p2d, wo, gpost, gfpre, wgu, wd, gfpost, gple, wpg, wpp)


def _rel_bucket_np(dist):
    n = np.maximum(dist, 0)
    max_exact = REL_BUCKETS // 2
    nf = np.maximum(n, 1).astype(np.float32)
    large = max_exact + (np.log(nf / np.float32(max_exact)) / np.float32(math.log(REL_MAX_DIST / max_exact))
                         * np.float32(REL_BUCKETS - max_exact)).astype(np.int32)
    large = np.minimum(large, REL_BUCKETS - 1)
    return np.where(n < max_exact, n, large)


def _bias_tables(rel_bias, seq):
    blk = ATT_BLOCK
    key = np.arange(blk)[:, None]
    qry = np.arange(blk)[None, :]
    far = _rel_bucket_np(np.arange(blk + 1, max(seq, blk + 2)))
    assert (far == far[0]).all(), "bias must be constant beyond the previous chunk"
    tab = rel_bias.astype(jnp.float32)
    tab = (tab - tab[:, int(far[0])][:, None]) * LOG2E

    def table(bucket):
        onehot = (jnp.asarray(bucket)[None] == jnp.arange(REL_BUCKETS)[:, None, None]).astype(jnp.float32)
        return jnp.einsum("hb,bkq->hkq", tab, onehot, precision=lax.Precision.HIGHEST)

    return table(_rel_bucket_np(qry - key)), table(_rel_bucket_np(qry - key + blk))


def _split_w_in(w_in):
    rest = w_in[..., MAIN_COLS:]
    sizes = (IDX_DIM, IDX_HEADS, GROUP_WIDTH, GROUP_WIDTH, GROUP_WIDTH)
    offs = np.concatenate([[0], np.cumsum(sizes)])
    ik, iw, dq, dk, dv = (rest[..., offs[i]:offs[i + 1]] for i in range(len(sizes)))
    pad = jnp.zeros(w_in.shape[:-1] + (LANES - IDX_HEADS,), w_in.dtype)
    tail = jnp.concatenate([dq, dk, dv, ik, ik, iw, pad], axis=-1)
    return w_in[..., :MAIN_COLS].astype(jnp.bfloat16), tail.astype(jnp.bfloat16)


def _block_diag(pool_w):
    depth, ng, g, _ = pool_w.shape
    eye = jnp.eye(ng, dtype=pool_w.dtype)
    bd = jnp.einsum("lgcd,gh->lgchd", pool_w, eye).reshape(depth, ng * g, ng * g)
    return bd.astype(jnp.bfloat16)


@jax.jit
def kernel(x, p, rel_bias, g_mix_pre, w_in, conv_w, pool_w, pool_scale, w_out, g_mix_post, g_ffn_pre, w_gate_up, w_down, g_ffn_post, g_ple, w_ple_gate, w_ple_proj):
    batch, seq, d = x.shape
    depth = w_in.shape[0]
    n = batch * seq
    d_ff = w_down.shape[1]
    assert seq % ROW_TILE == 0 and d_ff % COL_CHUNK == 0
    bf16 = jnp.bfloat16

    tdiag, tprev = _bias_tables(rel_bias, seq)
    w_main, w_tail = _split_w_in(w_in)
    pool_bd = _block_diag(pool_w)
    wgu = w_gate_up.astype(bf16)
    wd = w_down.astype(bf16)
    wo = w_out.astype(bf16)
    wpg = w_ple_gate.astype(bf16)
    wpp = w_ple_proj.astype(bf16)

    x2d = x.reshape(n, d)
    for i in range(depth):
        yab, zb, vt, zc, kmean = _inproj(x2d, g_mix_pre[i][None, :], w_main[i], w_tail[i], conv_w[i], pool_bd[i],
                                         pool_scale[i][None, :], batch, seq)
        ycd = _attention(zb, vt, zc, kmean.reshape(batch, seq // MOBA_BLOCK, GROUP_WIDTH),
                         tdiag, tprev, batch, seq)
        x2d = _tail(x2d, yab, ycd, p[i].reshape(n, -1), wo[i], g_mix_post[i][None, :],
                    g_ffn_pre[i][None, :], wgu[i], wd[i], g_ffn_post[i][None, :],
                    g_ple[i][None, :], wpg[i], wpp[i])
    return x2d.reshape(batch, seq, d)
```

```python
import functools
import math

import numpy as np
import jax
import jax.numpy as jnp
from jax import lax
from jax.experimental import pallas as pl
from jax.experimental.pallas import tpu as pltpu

HEAD_DIM = 64
GROUP_WIDTH = 256
CONV_WIDTH = 3
POOL_WINDOWS = (2, 4, 8, 16)
POOL_GROUP = GROUP_WIDTH // len(POOL_WINDOWS)
IDX_HEADS = 8
IDX_DIM = 64
DSA_TOPK_MAX = 256
MOBA_BLOCK = 256
MOBA_TOPB_MAX = 3
REL_BUCKETS = 32
REL_MAX_DIST = 128
N_HEADS = 8
RMS_EPS = 1e-6

LANES = 128
SUBLANES = 8
ATT_BLOCK = 256
FAR_GROUP = 3
SCORE_GROUP = 3
HALO = 16
NEG = -1e30
LOG2E = math.log2(math.e)
ACC_ROWS = HEAD_DIM + 16
INT_MIN = -(2 ** 31)
VMEM_LIMIT = 56 * 1024 * 1024
ROW_TILE = 512
TAIL_SUB = 256
FFN_CHUNK = 1024

gw_ = GROUP_WIDTH
SRC = dict(a_in=0, a_c=gw_, a_b=2 * gw_, pool_v=3 * gw_, cq=4 * gw_, ck=5 * gw_, cv=6 * gw_, iq=7 * gw_)
MAIN_COLS = 7 * gw_ + IDX_HEADS * IDX_DIM
TAIL = dict(dq=0, dk=gw_, dv=2 * gw_, ik2=3 * gw_, iw=3 * gw_ + LANES)
TAIL_COLS = 3 * gw_ + 2 * LANES
W8 = N_HEADS * HEAD_DIM
ZB_COLS = 2 * W8 + IDX_HEADS * IDX_DIM + 2 * IDX_DIM
ZV_COLS = W8
ZC_COLS = LANES
COL_CHUNK = 256
QSCALE = LOG2E * HEAD_DIM ** -0.5


def _rms(x, g):
    return x * lax.rsqrt(jnp.mean(x * x, axis=-1, keepdims=True) + RMS_EPS) * g


def _dot(a, b):
    return jnp.dot(a, b, preferred_element_type=jnp.float32)


def _dot_t(a, b):
    return lax.dot_general(a, b, (((1,), (1,)), ((), ())), preferred_element_type=jnp.float32)


def _inproj_kernel(x_ref, xprev_ref, g_ref, w_ref, wt_ref, cw_ref, pw_ref, ps_ref, yab_ref, zb_ref,
                   vt_ref, zc_ref, kmean_ref, hbuf, vbuf, *, tiles_per_seq):
    tm = x_ref.shape[0]
    blk = ATT_BLOCK
    gw = GROUP_WIDTH
    h = _rms(x_ref[...], g_ref[...]).astype(jnp.bfloat16)
    seq_tile = pl.program_id(0) % tiles_per_seq
    hp = _rms(xprev_ref[...], g_ref[...]).astype(jnp.bfloat16)
    prev = [_dot(hp, w_ref[:, SRC[name]:SRC[name] + gw]) for name in ("a_in", "a_c", "pool_v")]
    halo_h = jnp.where(seq_tile == 0, 0.0, prev[1] * prev[0])
    halo_v = jnp.where(seq_tile == 0, 0.0, prev[2])

    def main(name, off=0, width=gw):
        c0 = SRC[name] + off
        return _dot(h, w_ref[:, c0:c0 + width])

    def tail(name, width=gw):
        return _dot(h, wt_ref[:, TAIL[name]:TAIL[name] + width])

    mixer_in = [main("a_in"), main("a_c"), main("a_b"), main("pool_v")]
    zb_ref[:, 0:gw] = (main("cq") * QSCALE).astype(jnp.bfloat16)
    zb_ref[:, gw:2 * gw] = (tail("dq") * QSCALE).astype(jnp.bfloat16)
    zb_ref[:, 2 * gw:3 * gw] = main("ck").astype(jnp.bfloat16)
    dk = tail("dk")
    zb_ref[:, 3 * gw:4 * gw] = dk.astype(jnp.bfloat16)
    for r in range(tm // MOBA_BLOCK):
        kmean_ref[r] = jnp.mean(dk[r * MOBA_BLOCK:(r + 1) * MOBA_BLOCK], axis=0, keepdims=True)
    for c0 in range(0, IDX_HEADS * IDX_DIM, gw):
        zb_ref[:, 2 * W8 + c0:2 * W8 + c0 + gw] = main("iq", c0).astype(jnp.bfloat16)
    zb_ref[:, ZB_COLS - LANES:ZB_COLS] = tail("ik2", LANES).astype(jnp.bfloat16)
    for c0, z in ((0, main("cv")), (gw, tail("dv"))):
        for r in range(tm // blk):
            vt_ref[0, r, c0:c0 + gw, :] = z[r * blk:(r + 1) * blk, :].T.astype(jnp.bfloat16)
    zc_ref[...] = tail("iw", LANES)
    _conv_pool(*mixer_in, halo_h, halo_v, seq_tile, cw_ref, pw_ref, ps_ref, yab_ref, hbuf, vbuf)


def _inproj(layer, x2d, g, w, wt, conv_w, pool_w_bd, pool_scale, batch, seq):
    n, d = x2d.shape
    tm = ROW_TILE
    blk = ATT_BLOCK
    tiles_per_seq = seq // tm
    const = lambda i: (0, 0)
    return pl.pallas_call(
        functools.partial(_inproj_kernel, tiles_per_seq=tiles_per_seq),
        grid=(n // tm,),
        in_specs=[
            pl.BlockSpec((tm, d), lambda i: (i, 0)),
            pl.BlockSpec((HALO, d), lambda i: (jnp.maximum(i * (tm // HALO) - 1, 0), 0)),
            pl.BlockSpec((1, d), const),
            _layer_block(w, layer, pipeline_mode=pl.Buffered(1)),
            _layer_block(wt, layer, pipeline_mode=pl.Buffered(1)),
            pl.BlockSpec((CONV_WIDTH, GROUP_WIDTH), const),
            pl.BlockSpec((GROUP_WIDTH, GROUP_WIDTH), const),
            pl.BlockSpec((1, GROUP_WIDTH), const),
        ],
        out_specs=[
            pl.BlockSpec((tm, 2 * GROUP_WIDTH), lambda i: (i, 0)),
            pl.BlockSpec((tm, ZB_COLS), lambda i: (i, 0)),
            pl.BlockSpec((1, tm // blk, ZV_COLS, blk),
                         lambda i: (i // tiles_per_seq, i % tiles_per_seq, 0, 0)),
            pl.BlockSpec((tm, ZC_COLS), lambda i: (i, 0)),
            pl.BlockSpec((tm // MOBA_BLOCK, 1, GROUP_WIDTH), lambda i: (i, 0, 0)),
        ],
        out_shape=[
            jax.ShapeDtypeStruct((n, 2 * GROUP_WIDTH), jnp.bfloat16),
            jax.ShapeDtypeStruct((n, ZB_COLS), jnp.bfloat16),
            jax.ShapeDtypeStruct((batch, seq // blk, ZV_COLS, blk), jnp.bfloat16),
            jax.ShapeDtypeStruct((n, ZC_COLS), jnp.float32),
            jax.ShapeDtypeStruct((n // MOBA_BLOCK, 1, GROUP_WIDTH), jnp.float32),
        ],
        scratch_shapes=[pltpu.VMEM((HALO + tm, GROUP_WIDTH), jnp.float32),
                        pltpu.VMEM((HALO + tm, GROUP_WIDTH), jnp.float32)],
        compiler_params=pltpu.CompilerParams(
            dimension_semantics=("arbitrary",), vmem_limit_bytes=VMEM_LIMIT),
    )(x2d, x2d, g, w, wt, conv_w, pool_w_bd, pool_scale)


def _conv_pool(a_in, a_c, a_b, v, halo_h, halo_v, i, cw_ref, pw_ref, ps_ref, y_ref, hbuf, vbuf):
    ts = a_in.shape[0]
    gw = GROUP_WIDTH
    hbuf[0:HALO, :] = halo_h
    vbuf[0:HALO, :] = halo_v
    hbuf[HALO:HALO + ts, :] = a_c * a_in
    vbuf[HALO:HALO + ts, :] = v

    def hist(buf, d, lo, hi):
        return buf[HALO - d:HALO - d + ts, lo:hi]

    conv = (cw_ref[0:1, :] * hist(hbuf, 2, 0, gw) + cw_ref[1:2, :] * hist(hbuf, 1, 0, gw)
            + cw_ref[2:3, :] * hist(hbuf, 0, 0, gw))
    y_ref[:, 0:gw] = (a_b * conv).astype(y_ref.dtype)

    t_pos = i * ts + lax.broadcasted_iota(jnp.int32, (ts, LANES), 0)
    lane = lax.broadcasted_iota(jnp.int32, (ts, LANES), 1)
    first_group = lane < POOL_GROUP
    halves = []
    for half, (w_small, w_big) in enumerate(((2, 4), (8, 16))):
        lo, hi = half * LANES, (half + 1) * LANES
        acc = hist(vbuf, 0, lo, hi)
        for d in range(1, w_small):
            acc = acc + hist(vbuf, d, lo, hi)
        s_small = acc
        for d in range(w_small, w_big):
            acc = acc + hist(vbuf, d, lo, hi)
        wsum = jnp.where(first_group, s_small, acc)
        cnt = jnp.minimum(t_pos + 1, jnp.where(first_group, w_small, w_big)).astype(jnp.float32)
        halves.append(wsum / cnt - hist(vbuf, 0, lo, hi))
    dmat = jnp.concatenate(halves, axis=1).astype(jnp.bfloat16)
    yb = _dot(dmat, pw_ref[...]) * ps_ref[...]
    y_ref[:, gw:2 * gw] = yb.astype(y_ref.dtype)


def _attn_kernel(q_ref, k_ref, vt_ref, iq_ref, ik_ref, iw_ref, kmean_ref, tdiag_ref, tprev_ref,
                 o_ref, keys_scr, hi_scr, lo_scr, gmax_scr, mb_scr, moba_scr, qpad_scr, m_scr,
                 acc_scr, *, topk):
    qi = pl.program_id(1)
    blk = ATT_BLOCK
    hd = HEAD_DIM
    n_chunks = qi + 1
    key_idx = lax.broadcasted_iota(jnp.int32, (blk, blk), 0)
    qry_idx = lax.broadcasted_iota(jnp.int32, (blk, blk), 1)
    causal = key_idx <= qry_idx

    def fold(x):
        return jnp.sum(x.reshape(blk // SUBLANES, SUBLANES, blk), axis=0)

    iq = iq_ref[...]
    idx_scale = (IDX_HEADS ** -0.5) * (IDX_DIM ** -0.5)
    iw_t = iw_ref[...].T * idx_scale

    lane_q = lax.broadcasted_iota(jnp.int32, (blk, LANES), 1)

    def score_keys(j):
        ik2 = ik_ref[pl.ds(pl.multiple_of(j * blk, blk), blk), :]
        ik_half = [jnp.where(lane_q < IDX_DIM, ik2, jnp.zeros_like(ik2)),
                   jnp.where(lane_q >= IDX_DIM, ik2, jnp.zeros_like(ik2))]
        acc = jnp.zeros((blk, blk), jnp.float32)
        for h in range(IDX_HEADS):
            s = _dot_t(ik_half[h % 2], iq[:, (h // 2) * LANES:(h // 2 + 1) * LANES])
            acc = acc + jnp.maximum(s, 0.0) * iw_t[h:h + 1, :]
        return acc + 0.0

    def to_key(score):
        bits = pltpu.bitcast(score, jnp.int32)
        return jnp.where(bits < 0, bits ^ jnp.int32(0x7FFFFFFF), bits)

    def hi16(key):
        return lax.shift_right_arithmetic(key, 16).astype(jnp.int16)

    def lo16(key):
        return (key ^ 0x8000).astype(jnp.int16)

    def store_keys(j, key):
        keys_scr[j] = key
        hi_scr[j] = hi16(key)
        lo_scr[j] = lo16(key)

    def score_body(j, carry):
        score = score_keys(j)
        store_keys(j, to_key(score))
        gmax_scr[...] = jnp.maximum(gmax_scr[...], score)
        return carry

    gmax_scr[...] = jnp.full((blk, blk), -jnp.inf, jnp.float32)

    def score_group(i, carry):
        js = [SCORE_GROUP * i + c for c in range(SCORE_GROUP)]
        scores = [score_keys(j) for j in js]
        best = gmax_scr[...]
        for j, sc in zip(js, scores):
            store_keys(j, to_key(sc))
            best = jnp.maximum(best, sc)
        gmax_scr[...] = best
        return carry

    lax.fori_loop(0, qi // SCORE_GROUP, score_group, 0)
    lax.fori_loop(qi - qi % SCORE_GROUP, qi, score_body, 0)
    score = jnp.where(causal, score_keys(qi), -jnp.inf)
    store_keys(qi, jnp.where(causal, to_key(score), jnp.int32(INT_MIN)))
    gmax = jnp.maximum(gmax_scr[...], score)

    assert blk >= topk
    slot_min = jnp.min(gmax, axis=0, keepdims=True)
    lo_u = jnp.where(slot_min == -jnp.inf, jnp.int32(INT_MIN), to_key(slot_min)) ^ jnp.int32(INT_MIN)
    hi_u = to_key(jnp.max(gmax, axis=0, keepdims=True)) ^ jnp.int32(INT_MIN)
    open_bits = 32 - lax.clz(lo_u ^ hi_u)
    n_bits = jnp.max(open_bits.astype(jnp.float32)).astype(jnp.int32)
    low_mask = jnp.where(n_bits == 0, jnp.int32(0),
                         lax.shift_right_logical(jnp.int32(-1), (32 - n_bits) & 31))

    def count_keys(pred):
        def body(j, c):
            return c + fold(jnp.where(pred(keys_scr[j]), 1, 0))
        c = lax.fori_loop(0, n_chunks, body, jnp.zeros((SUBLANES, blk), jnp.int32))
        return jnp.sum(c, axis=0, keepdims=True)

    half_rows = 2 * SUBLANES

    def count_halves(ref, pred):
        def body(j, c):
            ge = jnp.where(pred(ref[j]), jnp.int16(1), jnp.int16(0))
            parts = [ge[r * half_rows:(r + 1) * half_rows] for r in range(blk // half_rows)]
            while len(parts) > 1:
                parts = [a + b for a, b in zip(parts[::2], parts[1::2])]
            return c + parts[0]
        c = lax.fori_loop(0, n_chunks, body, jnp.zeros((half_rows, blk), jnp.int16))
        return jnp.sum(c.astype(jnp.int32), axis=0, keepdims=True)

    def search_bits(first_it, last_it, count_ge, carry):
        def bit_body(it, carry):
            t_u, cnt_ge = carry
            cand_u = t_u | lax.shift_left(jnp.int32(1), 31 - it)
            cnt = count_ge(cand_u ^ jnp.int32(INT_MIN))
            take = cnt >= topk
            return jnp.where(take, cand_u, t_u), jnp.where(take, cnt, cnt_ge)
        return lax.fori_loop(first_it, last_it, bit_body, carry)

    t_u0 = lo_u & ~low_mask
    cand0 = t_u0 ^ jnp.int32(INT_MIN)
    carry = (t_u0, count_keys(lambda kj: kj >= cand0))
    first_it = 32 - n_bits
    carry = search_bits(first_it, 16, lambda cand: count_halves(hi_scr, lambda v: v >= hi16(cand)), carry)
    thr_hi = hi16(carry[0] ^ jnp.int32(INT_MIN))
    above = count_halves(hi_scr, lambda v: v > thr_hi)

    def open_body(j, c):
        lo_scr[j] = jnp.where(hi_scr[j] == thr_hi, lo_scr[j], jnp.int16(-0x8000))
        return c
    lax.fori_loop(0, n_chunks, open_body, 0)
    t_u, cnt_ge = search_bits(
        jnp.maximum(first_it, 16), 32,
        lambda cand: above + count_halves(lo_scr, lambda v: v >= lo16(cand)), carry)
    thr = t_u ^ jnp.int32(INT_MIN)
    tie_any = jnp.max(cnt_ge) > topk

    @pl.when(jnp.logical_not(tie_any))
    def _():
        def body(j, carry):
            mb_scr[j] = jnp.where(keys_scr[j] >= thr, 0.0, NEG)
            return carry
        lax.fori_loop(0, qi, body, 0)
        mb_scr[qi] = jnp.where((keys_scr[qi] >= thr) & causal, 0.0, NEG)

    @pl.when(tie_any)
    def _():
        need = (topk - count_keys(lambda kj: kj > thr)).astype(jnp.float32)
        lower = jnp.where(qry_idx <= key_idx, 1.0, 0.0).astype(jnp.bfloat16)

        def sel_chunk(j, base):
            kj = keys_scr[j]
            eq = kj == thr
            pref = _dot(lower, jnp.where(eq, 1.0, 0.0).astype(jnp.bfloat16)) + base
            return (kj > thr) | (eq & (pref <= need)), pref[blk - 1:blk, :]

        def body(j, base):
            sel, base = sel_chunk(j, base)
            mb_scr[j] = jnp.where(sel, 0.0, NEG)
            return base
        base = lax.fori_loop(0, qi, body, jnp.zeros((1, blk), jnp.float32))
        sel, _ = sel_chunk(qi, base)
        mb_scr[qi] = jnp.where(sel & causal, 0.0, NEG)

    q8 = q_ref[...]
    nb = kmean_ref.shape[1]
    blk_idx = lax.broadcasted_iota(jnp.int32, (nb, blk), 0)
    blk_idx_f = blk_idx.astype(jnp.float32)
    km = kmean_ref[0]
    km_hi = km.astype(jnp.bfloat16)
    km_lo = (km - km_hi.astype(jnp.float32)).astype(jnp.bfloat16)
    for h in range(4):
        qh = q8[:, (4 + h) * hd:(5 + h) * hd]
        gate = _dot_t(km_hi[:, h * hd:(h + 1) * hd], qh) + _dot_t(km_lo[:, h * hd:(h + 1) * hd], qh)
        gate = jnp.where(blk_idx < qi, gate, -jnp.inf)
        chosen = jnp.zeros((nb, blk), jnp.bool_)
        for _ in range(MOBA_TOPB_MAX):
            mx = jnp.max(gate, axis=0, keepdims=True)
            is_mx = (gate == mx) & (gate > -jnp.inf)
            first = jnp.min(jnp.where(is_mx, blk_idx_f, float(nb)), axis=0, keepdims=True)
            pick = blk_idx_f == first
            chosen = chosen | pick
            gate = jnp.where(pick, -jnp.inf, gate)
        moba_scr[h] = jnp.where(chosen, 0.0, NEG)

    m_scr[...] = jnp.full(m_scr.shape, NEG, jnp.float32)
    acc_scr[...] = jnp.zeros(acc_scr.shape, jnp.float32)
    ones_rows = jnp.ones((ACC_ROWS - hd, blk), jnp.bfloat16)

    for h in range(N_HEADS):
        pair = q8[:, (h // 2) * LANES:(h // 2 + 1) * LANES]
        qpad_scr[h] = jnp.where((lane_q >= hd) == (h % 2 == 1), pair, jnp.zeros_like(pair))

    def tile_rows(x, rows):
        return jnp.broadcast_to(x[None], (rows // SUBLANES, SUBLANES, blk)).reshape(rows, blk)

    def logits_phase(j, kind):
        kc = k_ref[pl.ds(pl.multiple_of(j * blk, blk), blk), :]
        staged = []
        for h in range(N_HEADS):
            s = _dot_t(kc[:, (h // 2) * LANES:(h // 2 + 1) * LANES], qpad_scr[h])
            if kind == "prev":
                s = s + tprev_ref[h]
            elif kind == "diag":
                s = s + tdiag_ref[h]
            if h < 4:
                s = s + mb_scr[j]
            elif kind == "diag":
                s = jnp.where(causal, s, NEG)
            mx = jnp.max(jnp.max(s.reshape(blk // SUBLANES, SUBLANES, blk), axis=0), axis=0, keepdims=True)
            m_prev = m_scr[h]
            m_new = jnp.maximum(m_prev, mx)
            shift = m_new
            if h >= 4 and kind != "diag":
                taken = moba_scr[h - 4, pl.ds(j, 1), :] == 0.0
                m_new = jnp.where(taken, m_new, m_prev)
                shift = jnp.where(taken, m_new, -NEG)
            m_scr[h] = m_new
            staged.append((s, shift, jnp.exp2(m_prev - m_new)))
        return staged

    def exp_phase(staged):
        return [(jnp.exp2(s - tile_rows(shift, blk)).astype(jnp.bfloat16), alpha)
                for s, shift, alpha in staged]

    def output_phase(j, probs):
        for h, (p, alpha) in enumerate(probs):
            vt_ones = jnp.concatenate([vt_ref[0, j, h * hd:(h + 1) * hd, :], ones_rows], axis=0)
            acc_scr[h] = tile_rows(alpha, ACC_ROWS) * acc_scr[h] + _dot(vt_ones, p)

    def attend(chunks):
        staged = [logits_phase(j, kind) for j, kind in chunks]
        probs = [exp_phase(st) for st in staged]
        for (j, _), pr in zip(chunks, probs):
            output_phase(j, pr)

    n_far = jnp.maximum(qi - 1, 0)

    def far_group(i, carry):
        attend([(FAR_GROUP * i + c, "far") for c in range(FAR_GROUP)])
        return carry

    lax.fori_loop(0, n_far // FAR_GROUP, far_group, 0)

    def far_single(j, carry):
        attend([(j, "far")])
        return carry

    lax.fori_loop(n_far - n_far % FAR_GROUP, n_far, far_single, 0)

    @pl.when(qi >= 1)
    def _():
        attend([(qi - 1, "prev"), (qi, "diag")])

    @pl.when(qi == 0)
    def _():
        attend([(qi, "diag")])

    out_t = jnp.concatenate(
        [acc_scr[h, 0:hd, :] / acc_scr[h, hd:hd + 1, :] for h in range(N_HEADS)], axis=0)
    o_ref[...] = out_t.T.astype(o_ref.dtype)


def _attention(zb, vt, zc, kmean, tdiag, tprev, batch, seq):
    blk = ATT_BLOCK
    nq = seq // blk
    nb = kmean.shape[1]
    topk = min(DSA_TOPK_MAX, seq // 4)
    kernel = functools.partial(_attn_kernel, topk=topk)
    resident = dict(pipeline_mode=pl.Buffered(1))
    return pl.pallas_call(
        kernel,
        grid=(batch, nq),
        in_specs=[
            pl.BlockSpec((blk, W8), lambda b, i: (b * nq + i, 0)),
            pl.BlockSpec((seq, W8), lambda b, i: (b, 1)),
            pl.BlockSpec((1, nq, W8, blk), lambda b, i: (b, 0, 0, 0)),
            pl.BlockSpec((blk, W8), lambda b, i: (b * nq + i, 2)),
            pl.BlockSpec((seq, LANES), lambda b, i: (b, 3 * W8 // LANES)),
            pl.BlockSpec((blk, LANES), lambda b, i: (b * nq + i, 0)),
            pl.BlockSpec((1, nb, GROUP_WIDTH), lambda b, i: (b, 0, 0)),
            pl.BlockSpec((N_HEADS, blk, blk), lambda b, i: (0, 0, 0), **resident),
            pl.BlockSpec((N_HEADS, blk, blk), lambda b, i: (0, 0, 0), **resident),
        ],
        out_specs=pl.BlockSpec((blk, W8), lambda b, i: (b * nq + i, 0)),
        out_shape=jax.ShapeDtypeStruct((batch * seq, W8), jnp.bfloat16),
        scratch_shapes=[
            pltpu.VMEM((nq, blk, blk), jnp.int32),
            pltpu.VMEM((nq, blk, blk), jnp.int16),
            pltpu.VMEM((nq, blk, blk), jnp.int16),
            pltpu.VMEM((blk, blk), jnp.float32),
            pltpu.VMEM((nq, blk, blk), jnp.float32),
            pltpu.VMEM((4, nb, blk), jnp.float32),
            pltpu.VMEM((N_HEADS, blk, LANES), jnp.bfloat16),
            pltpu.VMEM((N_HEADS, SUBLANES, blk), jnp.float32),
            pltpu.VMEM((N_HEADS, ACC_ROWS, blk), jnp.float32),
        ],
        compiler_params=pltpu.CompilerParams(
            dimension_semantics=("arbitrary", "arbitrary"), vmem_limit_bytes=VMEM_LIMIT),
    )(zb, zb, vt, zb, zb, zc, kmean, tdiag, tprev)


def _tail_kernel(x_ref, yab_ref, ycd_ref, p_ref, wo_ref, gpost_ref, gfpre_ref, wgu_ref, wd_ref,
                 gfpost_ref, gple_ref, wpg_ref, wpp_ref, o_ref, act_scr):
    tm = x_ref.shape[0]
    half = yab_ref.shape[1]
    d_ff = wd_ref.shape[0]
    subs = [slice(r * TAIL_SUB, (r + 1) * TAIL_SUB) for r in range(tm // TAIL_SUB)]

    mix = [_dot(yab_ref[r, :], wo_ref[0:half, :]) + _dot(ycd_ref[r, :], wo_ref[half:2 * half, :])
           for r in subs]
    x1 = [x_ref[r, :] + _rms(m, gpost_ref[...]) for r, m in zip(subs, mix)]
    h2 = [_rms(x, gfpre_ref[...]).astype(jnp.bfloat16) for x in x1]
    for c0 in range(0, d_ff, FFN_CHUNK):
        cw = min(FFN_CHUNK, d_ff - c0)
        gates = [_dot(h, wgu_ref[:, c0:c0 + cw]) for h in h2]
        ups = [_dot(h, wgu_ref[:, d_ff + c0:d_ff + c0 + cw]) for h in h2]
        for r, g, u in zip(subs, gates, ups):
            act_scr[r, c0:c0 + cw] = (g * jax.nn.sigmoid(g) * u).astype(jnp.bfloat16)
    f = [_dot(act_scr[r, :], wd_ref[...]) for r in subs]
    x2 = [x + _rms(y, gfpost_ref[...]) for x, y in zip(x1, f)]
    hg = [_rms(x, gple_ref[...]).astype(jnp.bfloat16) for x in x2]
    gate = [jax.nn.sigmoid(_dot(h, wpg_ref[...])) for h in hg]
    proj = [_dot(p_ref[r, :].astype(jnp.bfloat16), wpp_ref[...]) for r in subs]
    for r, x, g, pr in zip(subs, x2, gate, proj):
        o_ref[r, :] = x + g * pr


def _layer_block(stacked, layer, **kwargs):
    zeros = (0,) * (stacked.ndim - 1)
    return pl.BlockSpec((None,) + stacked.shape[1:], lambda i: (layer,) + zeros, **kwargs)


def _tail(layer, x2d, yab, ycd, p3d, wo, gpost, gfpre, wgu, wd, gfpost, gple, wpg, wpp):
    n, d = x2d.shape
    tm = ROW_TILE
    row = lambda i: (i, 0)
    resident = dict(pipeline_mode=pl.Buffered(1))
    vec = pl.BlockSpec((1, d), lambda i: (0, 0))
    return pl.pallas_call(
        _tail_kernel,
        grid=(n // tm,),
        in_specs=[
            pl.BlockSpec((tm, d), row),
            pl.BlockSpec((tm, yab.shape[1]), row),
            pl.BlockSpec((tm, ycd.shape[1]), row),
            pl.BlockSpec((None, tm, p3d.shape[2]), lambda i: (layer, i, 0)),
            _layer_block(wo, layer, **resident),
            vec, vec,
            _layer_block(wgu, layer, **resident),
            _layer_block(wd, layer, **resident),
            vec, vec,
            _layer_block(wpg, layer, **resident),
            _layer_block(wpp, layer, **resident),
        ],
        out_specs=pl.BlockSpec((tm, d), row),
        out_shape=jax.ShapeDtypeStruct((n, d), jnp.float32),
        scratch_shapes=[pltpu.VMEM((tm, wd.shape[1]), jnp.bfloat16)],
        compiler_params=pltpu.CompilerParams(
            dimension_semantics=("arbitrary",), vmem_limit_bytes=VMEM_LIMIT),
    )(x2d, yab, ycd, p3d, wo, gpost, gfpre, wgu, wd, gfpost, gple, wpg, wpp)


def _rel_bucket_np(dist):
    n = np.maximum(dist, 0)
    max_exact = REL_BUCKETS // 2
    nf = np.maximum(n, 1).astype(np.float32)
    large = max_exact + (np.log(nf / np.float32(max_exact)) / np.float32(math.log(REL_MAX_DIST / max_exact))
                         * np.float32(REL_BUCKETS - max_exact)).astype(np.int32)
    large = np.minimum(large, REL_BUCKETS - 1)
    return np.where(n < max_exact, n, large)


def _bias_tables(rel_bias, seq):
    blk = ATT_BLOCK
    key = np.arange(blk)[:, None]
    qry = np.arange(blk)[None, :]
    far = _rel_bucket_np(np.arange(blk + 1, max(seq, blk + 2)))
    assert (far == far[0]).all(), "bias must be constant beyond the previous chunk"
    tab = rel_bias.astype(jnp.float32)
    tab = (tab - tab[:, int(far[0])][:, None]) * LOG2E

    def table(bucket):
        onehot = (jnp.asarray(bucket)[None] == jnp.arange(REL_BUCKETS)[:, None, None]).astype(jnp.float32)
        return jnp.einsum("hb,bkq->hkq", tab, onehot, precision=lax.Precision.HIGHEST)

    return table(_rel_bucket_np(qry - key)), table(_rel_bucket_np(qry - key + blk))


def _split_w_in(w_in):
    rest = w_in[..., MAIN_COLS:]
    sizes = (IDX_DIM, IDX_HEADS, GROUP_WIDTH, GROUP_WIDTH, GROUP_WIDTH)
    offs = np.concatenate([[0], np.cumsum(sizes)])
    ik, iw, dq, dk, dv = (rest[..., offs[i]:offs[i + 1]] for i in range(len(sizes)))
    pad = jnp.zeros(w_in.shape[:-1] + (LANES - IDX_HEADS,), w_in.dtype)
    tail = jnp.concatenate([dq, dk, dv, ik, ik, iw, pad], axis=-1)
    return w_in[..., :MAIN_COLS].astype(jnp.bfloat16), tail.astype(jnp.bfloat16)


def _block_diag(pool_w):
    depth, ng, g, _ = pool_w.shape
    eye = jnp.eye(ng, dtype=pool_w.dtype)
    bd = jnp.einsum("lgcd,gh->lgchd", pool_w, eye).reshape(depth, ng * g, ng * g)
    return bd.astype(jnp.bfloat16)


@jax.jit
def kernel(x, p, rel_bias, g_mix_pre, w_in, conv_w, pool_w, pool_scale, w_out, g_mix_post, g_ffn_pre, w_gate_up, w_down, g_ffn_post, g_ple, w_ple_gate, w_ple_proj):
    batch, seq, d = x.shape
    depth = w_in.shape[0]
    n = batch * seq
    d_ff = w_down.shape[1]
    assert seq % ROW_TILE == 0 and d_ff % COL_CHUNK == 0
    bf16 = jnp.bfloat16

    tdiag, tprev = _bias_tables(rel_bias, seq)
    w_main, w_tail = _split_w_in(w_in)
    pool_bd = _block_diag(pool_w)
    wgu = w_gate_up.astype(bf16)
    wd = w_down.astype(bf16)
    wo = w_out.astype(bf16)
    wpg = w_ple_gate.astype(bf16)
    wpp = w_ple_proj.astype(bf16)

    x2d = x.reshape(n, d)
    p3d = p.reshape(depth, n, -1)
    for i in range(depth):
        yab, zb, vt, zc, kmean = _inproj(i, x2d, g_mix_pre[i][None, :], w_main, w_tail, conv_w[i],
                                         pool_bd[i], pool_scale[i][None, :], batch, seq)
        ycd = _attention(zb, vt, zc, kmean.reshape(batch, seq // MOBA_BLOCK, GROUP_WIDTH),
                         tdiag, tprev, batch, seq)
        x2d = _tail(i, x2d, yab, ycd, p3d, wo, g_mix_post[i][None, :], g_ffn_pre[i][None, :],
                    wgu, wd, g_ffn_post[i][None, :], g_ple[i][None, :], wpg, wpp)
    return x2d.reshape(batch, seq, d)
```

```python
import functools
import math

import numpy as np
import jax
import jax.numpy as jnp
from jax import lax
from jax.experimental import pallas as pl
from jax.experimental.pallas import tpu as pltpu

HEAD_DIM = 64
GROUP_WIDTH = 256
CONV_WIDTH = 3
POOL_WINDOWS = (2, 4, 8, 16)
POOL_GROUP = GROUP_WIDTH // len(POOL_WINDOWS)
IDX_HEADS = 8
IDX_DIM = 64
DSA_TOPK_MAX = 256
MOBA_BLOCK = 256
MOBA_TOPB_MAX = 3
REL_BUCKETS = 32
REL_MAX_DIST = 128
N_HEADS = 8
RMS_EPS = 1e-6

LANES = 128
SUBLANES = 8
ATT_BLOCK = 256
FAR_GROUP = 3
SCORE_GROUP = 3
HALO = 16
NEG = -1e30
LOG2E = math.log2(math.e)
ACC_ROWS = HEAD_DIM + 16
INT_MIN = -(2 ** 31)
WORD_BITS = 32
VMEM_LIMIT = 56 * 1024 * 1024
ROW_TILE = 512
TAIL_SUB = 256
FFN_CHUNK = 1024

gw_ = GROUP_WIDTH
SRC = dict(a_in=0, a_c=gw_, a_b=2 * gw_, pool_v=3 * gw_, cq=4 * gw_, ck=5 * gw_, cv=6 * gw_, iq=7 * gw_)
MAIN_COLS = 7 * gw_ + IDX_HEADS * IDX_DIM
TAIL = dict(dq=0, dk=gw_, dv=2 * gw_, ik2=3 * gw_, iw=3 * gw_ + LANES)
TAIL_COLS = 3 * gw_ + 2 * LANES
W8 = N_HEADS * HEAD_DIM
ZB_COLS = 2 * W8 + IDX_HEADS * IDX_DIM + 2 * IDX_DIM
ZV_COLS = W8
ZC_COLS = LANES
COL_CHUNK = 256
QSCALE = LOG2E * HEAD_DIM ** -0.5


def _rms(x, g):
    return x * lax.rsqrt(jnp.mean(x * x, axis=-1, keepdims=True) + RMS_EPS) * g


def _dot(a, b):
    return jnp.dot(a, b, preferred_element_type=jnp.float32)


def _dot_t(a, b):
    return lax.dot_general(a, b, (((1,), (1,)), ((), ())), preferred_element_type=jnp.float32)


def _inproj_kernel(x_ref, xprev_ref, g_ref, w_ref, wt_ref, cw_ref, pw_ref, ps_ref, yab_ref, zb_ref,
                   vt_ref, zc_ref, kmean_ref, hbuf, vbuf, *, tiles_per_seq):
    tm = x_ref.shape[0]
    blk = ATT_BLOCK
    gw = GROUP_WIDTH
    h = _rms(x_ref[...], g_ref[...]).astype(jnp.bfloat16)
    seq_tile = pl.program_id(0) % tiles_per_seq
    hp = _rms(xprev_ref[...], g_ref[...]).astype(jnp.bfloat16)
    prev = [_dot(hp, w_ref[:, SRC[name]:SRC[name] + gw]) for name in ("a_in", "a_c", "pool_v")]
    halo_h = jnp.where(seq_tile == 0, 0.0, prev[1] * prev[0])
    halo_v = jnp.where(seq_tile == 0, 0.0, prev[2])

    def main(name, off=0, width=gw):
        c0 = SRC[name] + off
        return _dot(h, w_ref[:, c0:c0 + width])

    def tail(name, width=gw):
        return _dot(h, wt_ref[:, TAIL[name]:TAIL[name] + width])

    mixer_in = [main("a_in"), main("a_c"), main("a_b"), main("pool_v")]
    zb_ref[:, 0:gw] = (main("cq") * QSCALE).astype(jnp.bfloat16)
    zb_ref[:, gw:2 * gw] = (tail("dq") * QSCALE).astype(jnp.bfloat16)
    zb_ref[:, 2 * gw:3 * gw] = main("ck").astype(jnp.bfloat16)
    dk = tail("dk")
    zb_ref[:, 3 * gw:4 * gw] = dk.astype(jnp.bfloat16)
    for r in range(tm // MOBA_BLOCK):
        kmean_ref[r] = jnp.mean(dk[r * MOBA_BLOCK:(r + 1) * MOBA_BLOCK], axis=0, keepdims=True)
    for c0 in range(0, IDX_HEADS * IDX_DIM, gw):
        zb_ref[:, 2 * W8 + c0:2 * W8 + c0 + gw] = main("iq", c0).astype(jnp.bfloat16)
    zb_ref[:, ZB_COLS - LANES:ZB_COLS] = tail("ik2", LANES).astype(jnp.bfloat16)
    for c0, z in ((0, main("cv")), (gw, tail("dv"))):
        for r in range(tm // blk):
            vt_ref[0, r, c0:c0 + gw, :] = z[r * blk:(r + 1) * blk, :].T.astype(jnp.bfloat16)
    zc_ref[...] = tail("iw", LANES)
    _conv_pool(*mixer_in, halo_h, halo_v, seq_tile, cw_ref, pw_ref, ps_ref, yab_ref, hbuf, vbuf)


def _inproj(layer, x2d, g, w, wt, conv_w, pool_w_bd, pool_scale, batch, seq):
    n, d = x2d.shape
    tm = ROW_TILE
    blk = ATT_BLOCK
    tiles_per_seq = seq // tm
    const = lambda i: (0, 0)
    return pl.pallas_call(
        functools.partial(_inproj_kernel, tiles_per_seq=tiles_per_seq),
        grid=(n // tm,),
        in_specs=[
            pl.BlockSpec((tm, d), lambda i: (i, 0)),
            pl.BlockSpec((HALO, d), lambda i: (jnp.maximum(i * (tm // HALO) - 1, 0), 0)),
            pl.BlockSpec((1, d), const),
            _layer_block(w, layer, pipeline_mode=pl.Buffered(1)),
            _layer_block(wt, layer, pipeline_mode=pl.Buffered(1)),
            pl.BlockSpec((CONV_WIDTH, GROUP_WIDTH), const),
            pl.BlockSpec((GROUP_WIDTH, GROUP_WIDTH), const),
            pl.BlockSpec((1, GROUP_WIDTH), const),
        ],
        out_specs=[
            pl.BlockSpec((tm, 2 * GROUP_WIDTH), lambda i: (i, 0)),
            pl.BlockSpec((tm, ZB_COLS), lambda i: (i, 0)),
            pl.BlockSpec((1, tm // blk, ZV_COLS, blk),
                         lambda i: (i // tiles_per_seq, i % tiles_per_seq, 0, 0)),
            pl.BlockSpec((tm, ZC_COLS), lambda i: (i, 0)),
            pl.BlockSpec((tm // MOBA_BLOCK, 1, GROUP_WIDTH), lambda i: (i, 0, 0)),
        ],
        out_shape=[
            jax.ShapeDtypeStruct((n, 2 * GROUP_WIDTH), jnp.bfloat16),
            jax.ShapeDtypeStruct((n, ZB_COLS), jnp.bfloat16),
            jax.ShapeDtypeStruct((batch, seq // blk, ZV_COLS, blk), jnp.bfloat16),
            jax.ShapeDtypeStruct((n, ZC_COLS), jnp.float32),
            jax.ShapeDtypeStruct((n // MOBA_BLOCK, 1, GROUP_WIDTH), jnp.float32),
        ],
        scratch_shapes=[pltpu.VMEM((HALO + tm, GROUP_WIDTH), jnp.float32),
                        pltpu.VMEM((HALO + tm, GROUP_WIDTH), jnp.float32)],
        compiler_params=pltpu.CompilerParams(
            dimension_semantics=("arbitrary",), vmem_limit_bytes=VMEM_LIMIT),
    )(x2d, x2d, g, w, wt, conv_w, pool_w_bd, pool_scale)


def _conv_pool(a_in, a_c, a_b, v, halo_h, halo_v, i, cw_ref, pw_ref, ps_ref, y_ref, hbuf, vbuf):
    ts = a_in.shape[0]
    gw = GROUP_WIDTH
    hbuf[0:HALO, :] = halo_h
    vbuf[0:HALO, :] = halo_v
    hbuf[HALO:HALO + ts, :] = a_c * a_in
    vbuf[HALO:HALO + ts, :] = v

    def hist(buf, d, lo, hi):
        return buf[HALO - d:HALO - d + ts, lo:hi]

    conv = (cw_ref[0:1, :] * hist(hbuf, 2, 0, gw) + cw_ref[1:2, :] * hist(hbuf, 1, 0, gw)
            + cw_ref[2:3, :] * hist(hbuf, 0, 0, gw))
    y_ref[:, 0:gw] = (a_b * conv).astype(y_ref.dtype)

    t_pos = i * ts + lax.broadcasted_iota(jnp.int32, (ts, LANES), 0)
    lane = lax.broadcasted_iota(jnp.int32, (ts, LANES), 1)
    first_group = lane < POOL_GROUP
    halves = []
    for half, (w_small, w_big) in enumerate(((2, 4), (8, 16))):
        lo, hi = half * LANES, (half + 1) * LANES
        acc = hist(vbuf, 0, lo, hi)
        for d in range(1, w_small):
            acc = acc + hist(vbuf, d, lo, hi)
        s_small = acc
        for d in range(w_small, w_big):
            acc = acc + hist(vbuf, d, lo, hi)
        wsum = jnp.where(first_group, s_small, acc)
        cnt = jnp.minimum(t_pos + 1, jnp.where(first_group, w_small, w_big)).astype(jnp.float32)
        halves.append(wsum / cnt - hist(vbuf, 0, lo, hi))
    dmat = jnp.concatenate(halves, axis=1).astype(jnp.bfloat16)
    yb = _dot(dmat, pw_ref[...]) * ps_ref[...]
    y_ref[:, gw:2 * gw] = yb.astype(y_ref.dtype)


def _attn_kernel(q_ref, k_ref, vt_ref, iq_ref, ik_ref, iw_ref, kmean_ref, tdiag_ref, tprev_ref,
                 o_ref, keys_scr, planes_scr, alive_scr, mb_scr, moba_scr, qpad_scr, m_scr,
                 acc_scr, *, topk):
    qi = pl.program_id(1)
    blk = ATT_BLOCK
    hd = HEAD_DIM
    key_idx = lax.broadcasted_iota(jnp.int32, (blk, blk), 0)
    qry_idx = lax.broadcasted_iota(jnp.int32, (blk, blk), 1)
    causal = key_idx <= qry_idx

    iq = iq_ref[...]
    idx_scale = (IDX_HEADS ** -0.5) * (IDX_DIM ** -0.5)
    iw_t = iw_ref[...].T * idx_scale

    lane_q = lax.broadcasted_iota(jnp.int32, (blk, LANES), 1)

    def score_keys(j):
        ik2 = ik_ref[pl.ds(pl.multiple_of(j * blk, blk), blk), :]
        ik_half = [jnp.where(lane_q < IDX_DIM, ik2, jnp.zeros_like(ik2)),
                   jnp.where(lane_q >= IDX_DIM, ik2, jnp.zeros_like(ik2))]
        acc = jnp.zeros((blk, blk), jnp.float32)
        for h in range(IDX_HEADS):
            s = _dot_t(ik_half[h % 2], iq[:, (h // 2) * LANES:(h // 2 + 1) * LANES])
            acc = acc + jnp.maximum(s, 0.0) * iw_t[h:h + 1, :]
        return acc + 0.0

    def to_key(score):
        bits = pltpu.bitcast(score, jnp.int32)
        return jnp.where(bits < 0, bits ^ jnp.int32(0x7FFFFFFF), bits)

    def store_keys(j, key):
        keys_scr[j] = key
        words = (key ^ jnp.int32(INT_MIN)).reshape(WORD_BITS, blk // WORD_BITS, blk)
        rows = [words[v] for v in range(WORD_BITS)]
        step, mask = WORD_BITS // 2, 0x0000FFFF
        while step:
            k = 0
            while k < WORD_BITS:
                t = (rows[k] ^ lax.shift_right_logical(rows[k + step], jnp.int32(step))) & mask
                rows[k] = rows[k] ^ t
                rows[k + step] = rows[k + step] ^ lax.shift_left(t, jnp.int32(step))
                k = (k + step + 1) & ~step
            step //= 2
            mask = (mask ^ (mask << step)) & 0xFFFFFFFF if step else mask
        for bit, plane in enumerate(rows):
            planes_scr[bit, j] = plane

    def score_body(j, carry):
        store_keys(j, to_key(score_keys(j)))
        return carry

    def score_group(i, carry):
        js = [SCORE_GROUP * i + c for c in range(SCORE_GROUP)]
        scores = [score_keys(j) for j in js]
        for j, sc in zip(js, scores):
            store_keys(j, to_key(sc))
        return carry

    lax.fori_loop(0, qi // SCORE_GROUP, score_group, 0)
    lax.fori_loop(qi - qi % SCORE_GROUP, qi, score_body, 0)
    store_keys(qi, jnp.where(causal, to_key(score_keys(qi)), jnp.int32(INT_MIN)))

    @pl.when((pl.program_id(0) == 0) & (qi == 0))
    def _():
        planes_scr[...] = jnp.zeros(planes_scr.shape, jnp.int32)

    chunk_idx = lax.broadcasted_iota(jnp.int32, alive_scr.shape, 0)
    alive_scr[...] = jnp.where(chunk_idx <= qi, jnp.int32(-1), jnp.int32(0))

    def bit_body(it, carry):
        thr_u, need = carry
        alive = alive_scr[...]
        with_bit = alive & planes_scr[it]
        ones = jnp.sum(jnp.sum(lax.population_count(with_bit), axis=0), axis=0, keepdims=True)
        take = ones >= need
        alive_scr[...] = jnp.where(take, with_bit, alive ^ with_bit)
        thr_u = jnp.where(take, thr_u | lax.shift_left(jnp.int32(1), 31 - it), thr_u)
        return thr_u, jnp.where(take, need, need - ones)

    thr_u, need = lax.fori_loop(0, WORD_BITS, bit_body,
                                (jnp.zeros((1, blk), jnp.int32), jnp.full((1, blk), topk, jnp.int32)))
    thr = thr_u ^ jnp.int32(INT_MIN)
    n_equal = jnp.sum(jnp.sum(lax.population_count(alive_scr[...]), axis=0), axis=0, keepdims=True)
    tie_any = jnp.max((n_equal - need).astype(jnp.float32)) > 0.0

    @pl.when(jnp.logical_not(tie_any))
    def _():
        def body(j, carry):
            mb_scr[j] = jnp.where(keys_scr[j] >= thr, 0.0, NEG)
            return carry
        lax.fori_loop(0, qi, body, 0)
        mb_scr[qi] = jnp.where((keys_scr[qi] >= thr) & causal, 0.0, NEG)

    @pl.when(tie_any)
    def _():
        places = need.astype(jnp.float32)
        lower =jnp.where(qry_idx <= key_idx, 1.0, 0.0).astype(jnp.bfloat16)

        def sel_chunk(j, base):
            kj = keys_scr[j]
            eq = kj == thr
            pref = _dot(lower, jnp.where(eq, 1.0, 0.0).astype(jnp.bfloat16)) + base
            return (kj > thr) | (eq & (pref <= places)), pref[blk - 1:blk, :]

        def body(j, base):
            sel, base = sel_chunk(j, base)
            mb_scr[j] = jnp.where(sel, 0.0, NEG)
            return base
        base = lax.fori_loop(0, qi, body, jnp.zeros((1, blk), jnp.float32))
        sel, _ = sel_chunk(qi, base)
        mb_scr[qi] = jnp.where(sel & causal, 0.0, NEG)

    q8 = q_ref[...]
    nb = kmean_ref.shape[1]
    blk_idx = lax.broadcasted_iota(jnp.int32, (nb, blk), 0)
    blk_idx_f = blk_idx.astype(jnp.float32)
    km = kmean_ref[0]
    km_hi = km.astype(jnp.bfloat16)
    km_lo = (km - km_hi.astype(jnp.float32)).astype(jnp.bfloat16)
    for h in range(4):
        qh = q8[:, (4 + h) * hd:(5 + h) * hd]
        gate = _dot_t(km_hi[:, h * hd:(h + 1) * hd], qh) + _dot_t(km_lo[:, h * hd:(h + 1) * hd], qh)
        gate = jnp.where(blk_idx < qi, gate, -jnp.inf)
        chosen = jnp.zeros((nb, blk), jnp.bool_)
        for _ in range(MOBA_TOPB_MAX):
            mx = jnp.max(gate, axis=0, keepdims=True)
            is_mx = (gate == mx) & (gate > -jnp.inf)
            first = jnp.min(jnp.where(is_mx, blk_idx_f, float(nb)), axis=0, keepdims=True)
            pick = blk_idx_f == first
            chosen = chosen | pick
            gate = jnp.where(pick, -jnp.inf, gate)
        moba_scr[h] = jnp.where(chosen, 0.0, NEG)

    m_scr[...] = jnp.full(m_scr.shape, NEG, jnp.float32)
    acc_scr[...] = jnp.zeros(acc_scr.shape, jnp.float32)
    ones_rows = jnp.ones((ACC_ROWS - hd, blk), jnp.bfloat16)

    for h in range(N_HEADS):
        pair = q8[:, (h // 2) * LANES:(h // 2 + 1) * LANES]
        qpad_scr[h] = jnp.where((lane_q >= hd) == (h % 2 == 1), pair, jnp.zeros_like(pair))

    def tile_rows(x, rows):
        return jnp.broadcast_to(x[None], (rows // SUBLANES, SUBLANES, blk)).reshape(rows, blk)

    def logits_phase(j, kind):
        kc = k_ref[pl.ds(pl.multiple_of(j * blk, blk), blk), :]
        staged = []
        for h in range(N_HEADS):
            s = _dot_t(kc[:, (h // 2) * LANES:(h // 2 + 1) * LANES], qpad_scr[h])
            if kind == "prev":
                s = s + tprev_ref[h]
            elif kind == "diag":
                s = s + tdiag_ref[h]
            if h < 4:
                s = s + mb_scr[j]
            elif kind == "diag":
                s = jnp.where(causal, s, NEG)
            mx = jnp.max(jnp.max(s.reshape(blk // SUBLANES, SUBLANES, blk), axis=0), axis=0, keepdims=True)
            m_prev = m_scr[h]
            m_new = jnp.maximum(m_prev, mx)
            shift = m_new
            if h >= 4 and kind != "diag":
                taken = moba_scr[h - 4, pl.ds(j, 1), :] == 0.0
                m_new = jnp.where(taken, m_new, m_prev)
                shift = jnp.where(taken, m_new, -NEG)
            m_scr[h] = m_new
            staged.append((s, shift, jnp.exp2(m_prev - m_new)))
        return staged

    def exp_phase(staged):
        return [(jnp.exp2(s - tile_rows(shift, blk)).astype(jnp.bfloat16), alpha)
                for s, shift, alpha in staged]

    def output_phase(j, probs):
        for h, (p, alpha) in enumerate(probs):
            vt_ones = jnp.concatenate([vt_ref[0, j, h * hd:(h + 1) * hd, :], ones_rows], axis=0)
            acc_scr[h] = tile_rows(alpha, ACC_ROWS) * acc_scr[h] + _dot(vt_ones, p)

    def attend(chunks):
        staged = [logits_phase(j, kind) for j, kind in chunks]
        probs = [exp_phase(st) for st in staged]
        for (j, _), pr in zip(chunks, probs):
            output_phase(j, pr)

    n_far = jnp.maximum(qi - 1, 0)

    def far_group(i, carry):
        attend([(FAR_GROUP * i + c, "far") for c in range(FAR_GROUP)])
        return carry

    lax.fori_loop(0, n_far // FAR_GROUP, far_group, 0)

    def far_single(j, carry):
        attend([(j, "far")])
        return carry

    lax.fori_loop(n_far - n_far % FAR_GROUP, n_far, far_single, 0)

    @pl.when(qi >= 1)
    def _():
        attend([(qi - 1, "prev"), (qi, "diag")])

    @pl.when(qi == 0)
    def _():
        attend([(qi, "diag")])

    out_t = jnp.concatenate(
        [acc_scr[h, 0:hd, :] / acc_scr[h, hd:hd + 1, :] for h in range(N_HEADS)], axis=0)
    o_ref[...] = out_t.T.astype(o_ref.dtype)


def _attention(zb, vt, zc, kmean, tdiag, tprev, batch, seq):
    blk = ATT_BLOCK
    nq = seq // blk
    nb = kmean.shape[1]
    topk = min(DSA_TOPK_MAX, seq // 4)
    kernel = functools.partial(_attn_kernel, topk=topk)
    resident = dict(pipeline_mode=pl.Buffered(1))
    return pl.pallas_call(
        kernel,
        grid=(batch, nq),
        in_specs=[
            pl.BlockSpec((blk, W8), lambda b, i: (b * nq + i, 0)),
            pl.BlockSpec((seq, W8), lambda b, i: (b, 1)),
            pl.BlockSpec((1, nq, W8, blk), lambda b, i: (b, 0, 0, 0)),
            pl.BlockSpec((blk, W8), lambda b, i: (b * nq + i, 2)),
            pl.BlockSpec((seq, LANES), lambda b, i: (b, 3 * W8 // LANES)),
            pl.BlockSpec((blk, LANES), lambda b, i: (b * nq + i, 0)),
            pl.BlockSpec((1, nb, GROUP_WIDTH), lambda b, i: (b, 0, 0)),
            pl.BlockSpec((N_HEADS, blk, blk), lambda b, i: (0, 0, 0), **resident),
            pl.BlockSpec((N_HEADS, blk, blk), lambda b, i: (0, 0, 0), **resident),
        ],
        out_specs=pl.BlockSpec((blk, W8), lambda b, i: (b * nq + i, 0)),
        out_shape=jax.ShapeDtypeStruct((batch * seq, W8), jnp.bfloat16),
        scratch_shapes=[
            pltpu.VMEM((nq, blk, blk), jnp.int32),
            pltpu.VMEM((WORD_BITS, nq, blk // WORD_BITS, blk), jnp.int32),
            pltpu.VMEM((nq, blk // WORD_BITS, blk), jnp.int32),
            pltpu.VMEM((nq, blk, blk), jnp.float32),
            pltpu.VMEM((4, nb, blk), jnp.float32),
            pltpu.VMEM((N_HEADS, blk, LANES), jnp.bfloat16),
            pltpu.VMEM((N_HEADS, SUBLANES, blk), jnp.float32),
            pltpu.VMEM((N_HEADS, ACC_ROWS, blk), jnp.float32),
        ],
        compiler_params=pltpu.CompilerParams(
            dimension_semantics=("arbitrary", "arbitrary"), vmem_limit_bytes=VMEM_LIMIT),
    )(zb, zb, vt, zb, zb, zc, kmean, tdiag, tprev)


def _tail_kernel(x_ref, yab_ref, ycd_ref, p_ref, wo_ref, gpost_ref, gfpre_ref, wgu_ref, wd_ref,
                 gfpost_ref, gple_ref, wpg_ref, wpp_ref, o_ref, act_scr):
    tm = x_ref.shape[0]
    half = yab_ref.shape[1]
    d_ff = wd_ref.shape[0]
    subs = [slice(r * TAIL_SUB, (r + 1) * TAIL_SUB) for r in range(tm // TAIL_SUB)]

    mix = [_dot(yab_ref[r, :], wo_ref[0:half, :]) + _dot(ycd_ref[r, :], wo_ref[half:2 * half, :])
           for r in subs]
    x1 = [x_ref[r, :] + _rms(m, gpost_ref[...]) for r, m in zip(subs, mix)]
    h2 = [_rms(x, gfpre_ref[...]).astype(jnp.bfloat16) for x in x1]
    for c0 in range(0, d_ff, FFN_CHUNK):
        cw = min(FFN_CHUNK, d_ff - c0)
        gates = [_dot(h, wgu_ref[:, c0:c0 + cw]) for h in h2]
        ups = [_dot(h, wgu_ref[:, d_ff + c0:d_ff + c0 + cw]) for h in h2]
        for r, g, u in zip(subs, gates, ups):
            act_scr[r, c0:c0 + cw] = (g * jax.nn.sigmoid(g) * u).astype(jnp.bfloat16)
    f = [_dot(act_scr[r, :], wd_ref[...]) for r in subs]
    x2 = [x + _rms(y, gfpost_ref[...]) for x, y in zip(x1, f)]
    hg = [_rms(x, gple_ref[...]).astype(jnp.bfloat16) for x in x2]
    gate = [jax.nn.sigmoid(_dot(h, wpg_ref[...])) for h in hg]
    proj = [_dot(p_ref[r, :].astype(jnp.bfloat16), wpp_ref[...]) for r in subs]
    for r, x, g, pr in zip(subs, x2, gate, proj):
        o_ref[r, :] = x + g * pr


def _layer_block(stacked, layer, **kwargs):
    zeros = (0,) * (stacked.ndim - 1)
    return pl.BlockSpec((None,) + stacked.shape[1:], lambda i: (layer,) + zeros, **kwargs)


def _tail(layer, x2d, yab, ycd, p3d, wo, gpost, gfpre, wgu, wd, gfpost, gple, wpg, wpp):
    n, d = x2d.shape
    tm = ROW_TILE
    row = lambda i: (i, 0)
    resident = dict(pipeline_mode=pl.Buffered(1))
    vec = pl.BlockSpec((1, d), lambda i: (0, 0))
    return pl.pallas_call(
        _tail_kernel,
        grid=(n // tm,),
        in_specs=[
            pl.BlockSpec((tm, d), row),
            pl.BlockSpec((tm, yab.shape[1]), row),
            pl.BlockSpec((tm, ycd.shape[1]), row),
            pl.BlockSpec((None, tm, p3d.shape[2]), lambda i: (layer, i, 0)),
            _layer_block(wo, layer, **resident),
            vec, vec,
            _layer_block(wgu, layer, **resident),
            _layer_block(wd, layer, **resident),
            vec, vec,
            _layer_block(wpg, layer, **resident),
            _layer_block(wpp, layer, **resident),
        ],
        out_specs=pl.BlockSpec((tm, d), row),
        out_shape=jax.ShapeDtypeStruct((n, d), jnp.float32),
        scratch_shapes=[pltpu.VMEM((tm, wd.shape[1]), jnp.bfloat16)],
        compiler_params=pltpu.CompilerParams(
            dimension_semantics=("arbitrary",), vmem_limit_bytes=VMEM_LIMIT),
    )(x2d, yab, ycd, p3d, wo, gpost, gfpre, wgu, wd, gfpost, gple, wpg, wpp)


def _rel_bucket_np(dist):
    n = np.maximum(dist, 0)
    max_exact = REL_BUCKETS // 2
    nf = np.maximum(n, 1).astype(np.float32)
    large = max_exact + (np.log(nf / np.float32(max_exact)) / np.float32(math.log(REL_MAX_DIST / max_exact))
                         * np.float32(REL_BUCKETS - max_exact)).astype(np.int32)
    large = np.minimum(large, REL_BUCKETS - 1)
    return np.where(n < max_exact, n, large)


def _bias_tables(rel_bias, seq):
    blk = ATT_BLOCK
    key = np.arange(blk)[:, None]
    qry = np.arange(blk)[None, :]
    far = _rel_bucket_np(np.arange(blk + 1, max(seq, blk + 2)))
    assert (far == far[0]).all(), "bias must be constant beyond the previous chunk"
    tab = rel_bias.astype(jnp.float32)
    tab = (tab - tab[:, int(far[0])][:, None]) * LOG2E

    def table(bucket):
        onehot = (jnp.asarray(bucket)[None] == jnp.arange(REL_BUCKETS)[:, None, None]).astype(jnp.float32)
        return jnp.einsum("hb,bkq->hkq", tab, onehot, precision=lax.Precision.HIGHEST)

    return table(_rel_bucket_np(qry - key)), table(_rel_bucket_np(qry - key + blk))


def _split_w_in(w_in):
    rest = w_in[..., MAIN_COLS:]
    sizes = (IDX_DIM, IDX_HEADS, GROUP_WIDTH, GROUP_WIDTH, GROUP_WIDTH)
    offs = np.concatenate([[0], np.cumsum(sizes)])
    ik, iw, dq, dk, dv = (rest[..., offs[i]:offs[i + 1]] for i in range(len(sizes)))
    pad = jnp.zeros(w_in.shape[:-1] + (LANES - IDX_HEADS,), w_in.dtype)
    tail = jnp.concatenate([dq, dk, dv, ik, ik, iw, pad], axis=-1)
    return w_in[..., :MAIN_COLS].astype(jnp.bfloat16), tail.astype(jnp.bfloat16)


def _block_diag(pool_w):
    depth, ng, g, _ = pool_w.shape
    eye = jnp.eye(ng, dtype=pool_w.dtype)
    bd = jnp.einsum("lgcd,gh->lgchd", pool_w, eye).reshape(depth, ng * g, ng * g)
    return bd.astype(jnp.bfloat16)


@jax.jit
def kernel(x, p, rel_bias, g_mix_pre, w_in, conv_w, pool_w, pool_scale, w_out, g_mix_post, g_ffn_pre, w_gate_up, w_down, g_ffn_post, g_ple, w_ple_gate, w_ple_proj):
    batch, seq, d = x.shape
    depth = w_in.shape[0]
    n = batch * seq
    d_ff = w_down.shape[1]
    assert seq % ROW_TILE == 0 and d_ff % COL_CHUNK == 0
    bf16 = jnp.bfloat16

    tdiag, tprev = _bias_tables(rel_bias, seq)
    w_main, w_tail = _split_w_in(w_in)
    pool_bd = _block_diag(pool_w)
    wgu = w_gate_up.astype(bf16)
    wd = w_down.astype(bf16)
    wo = w_out.astype(bf16)
    wpg = w_ple_gate.astype(bf16)
    wpp = w_ple_proj.astype(bf16)

    x2d = x.reshape(n, d)
    p3d = p.reshape(depth, n, -1)
    for i in range(depth):
        yab, zb, vt, zc, kmean = _inproj(i, x2d, g_mix_pre[i][None, :], w_main, w_tail, conv_w[i],
                                         pool_bd[i], pool_scale[i][None, :], batch, seq)
        ycd = _attention(zb, vt, zc, kmean.reshape(batch, seq // MOBA_BLOCK, GROUP_WIDTH),
                         tdiag, tprev, batch, seq)
        x2d = _tail(i, x2d, yab, ycd, p3d, wo, g_mix_post[i][None, :], g_ffn_pre[i][None, :],
                    wgu, wd, g_ffn_post[i][None, :], g_ple[i][None, :], wpg, wpp)
    return x2d.reshape(batch, seq, d)
```

```python
import functools
import math

import numpy as np
import jax
import jax.numpy as jnp
from jax import lax
from jax.experimental import pallas as pl
from jax.experimental.pallas import tpu as pltpu

HEAD_DIM = 64
GROUP_WIDTH = 256
CONV_WIDTH = 3
POOL_WINDOWS = (2, 4, 8, 16)
POOL_GROUP = GROUP_WIDTH // len(POOL_WINDOWS)
IDX_HEADS = 8
IDX_DIM = 64
DSA_TOPK_MAX = 256
MOBA_BLOCK = 256
MOBA_TOPB_MAX = 3
REL_BUCKETS = 32
REL_MAX_DIST = 128
N_HEADS = 8
RMS_EPS = 1e-6

LANES = 128
SUBLANES = 8
ATT_BLOCK = 256
FAR_GROUP = 4
SCORE_GROUP = 4
HALO = 16
NEG = -1e30
LOG2E = math.log2(math.e)
ACC_ROWS = HEAD_DIM + 16
INT_MIN = -(2 ** 31)
WORD_BITS = 32
VMEM_LIMIT = 56 * 1024 * 1024
ROW_TILE = 512
TAIL_SUB = 256
FFN_CHUNK = 1024

gw_ = GROUP_WIDTH
SRC = dict(a_in=0, a_c=gw_, a_b=2 * gw_, pool_v=3 * gw_, cq=4 * gw_, ck=5 * gw_, cv=6 * gw_, iq=7 * gw_)
MAIN_COLS = 7 * gw_ + IDX_HEADS * IDX_DIM
TAIL = dict(dq=0, dk=gw_, dv=2 * gw_, ik2=3 * gw_, iw=3 * gw_ + LANES)
TAIL_COLS = 3 * gw_ + 2 * LANES
W8 = N_HEADS * HEAD_DIM
ZB_COLS = 2 * W8 + IDX_HEADS * IDX_DIM + 2 * IDX_DIM
ZV_COLS = W8
ZC_COLS = LANES
QSCALE = LOG2E * HEAD_DIM ** -0.5


def _rms(x, g):
    return x * lax.rsqrt(jnp.mean(x * x, axis=-1, keepdims=True) + RMS_EPS) * g


def _dot(a, b):
    return jnp.dot(a, b, preferred_element_type=jnp.float32)


def _dot_t(a, b):
    return lax.dot_general(a, b, (((1,), (1,)), ((), ())), preferred_element_type=jnp.float32)


def _inproj_kernel(x_ref, xprev_ref, g_ref, w_ref, wt_ref, cw_ref, pw_ref, ps_ref, yab_ref, zb_ref,
                   vt_ref, zc_ref, kmean_ref, hbuf, vbuf, *, tiles_per_seq):
    tm = x_ref.shape[0]
    blk = ATT_BLOCK
    gw = GROUP_WIDTH
    h = _rms(x_ref[...], g_ref[...]).astype(jnp.bfloat16)
    seq_tile = pl.program_id(0) % tiles_per_seq
    hp = _rms(xprev_ref[...], g_ref[...]).astype(jnp.bfloat16)
    prev = [_dot(hp, w_ref[:, SRC[name]:SRC[name] + gw]) for name in ("a_in", "a_c", "pool_v")]
    halo_h = jnp.where(seq_tile == 0, 0.0, prev[1] * prev[0])
    halo_v = jnp.where(seq_tile == 0, 0.0, prev[2])

    def main(name, off=0, width=gw):
        c0 = SRC[name] + off
        return _dot(h, w_ref[:, c0:c0 + width])

    def tail(name, width=gw):
        return _dot(h, wt_ref[:, TAIL[name]:TAIL[name] + width])

    mixer_in = [main("a_in"), main("a_c"), main("a_b"), main("pool_v")]
    zb_ref[:, 0:gw] = (main("cq") * QSCALE).astype(jnp.bfloat16)
    zb_ref[:, gw:2 * gw] = (tail("dq") * QSCALE).astype(jnp.bfloat16)
    zb_ref[:, 2 * gw:3 * gw] = main("ck").astype(jnp.bfloat16)
    dk = tail("dk")
    zb_ref[:, 3 * gw:4 * gw] = dk.astype(jnp.bfloat16)
    for r in range(tm // MOBA_BLOCK):
        kmean_ref[r] = jnp.mean(dk[r * MOBA_BLOCK:(r + 1) * MOBA_BLOCK], axis=0, keepdims=True)
    for c0 in range(0, IDX_HEADS * IDX_DIM, gw):
        zb_ref[:, 2 * W8 + c0:2 * W8 + c0 + gw] = main("iq", c0).astype(jnp.bfloat16)
    zb_ref[:, ZB_COLS - LANES:ZB_COLS] = tail("ik2", LANES).astype(jnp.bfloat16)
    for c0, z in ((0, main("cv")), (gw, tail("dv"))):
        for r in range(tm // blk):
            vt_ref[0, r, c0:c0 + gw, :] = z[r * blk:(r + 1) * blk, :].T.astype(jnp.bfloat16)
    zc_ref[...] = tail("iw", LANES)
    _conv_pool(*mixer_in, halo_h, halo_v, seq_tile, cw_ref, pw_ref, ps_ref, yab_ref, hbuf, vbuf)


def _inproj(layer, x2d, g, w, wt, conv_w, pool_w_bd, pool_scale, batch, seq):
    n, d = x2d.shape
    tm = ROW_TILE
    blk = ATT_BLOCK
    tiles_per_seq = seq // tm
    const = lambda i: (0, 0)
    return pl.pallas_call(
        functools.partial(_inproj_kernel, tiles_per_seq=tiles_per_seq),
        grid=(n // tm,),
        in_specs=[
            pl.BlockSpec((tm, d), lambda i: (i, 0)),
            pl.BlockSpec((HALO, d), lambda i: (jnp.maximum(i * (tm // HALO) - 1, 0), 0)),
            pl.BlockSpec((1, d), const),
            _layer_block(w, layer, pipeline_mode=pl.Buffered(1)),
            _layer_block(wt, layer, pipeline_mode=pl.Buffered(1)),
            pl.BlockSpec((CONV_WIDTH, GROUP_WIDTH), const),
            pl.BlockSpec((GROUP_WIDTH, GROUP_WIDTH), const),
            pl.BlockSpec((1, GROUP_WIDTH), const),
        ],
        out_specs=[
            pl.BlockSpec((tm, 2 * GROUP_WIDTH), lambda i: (i, 0)),
            pl.BlockSpec((tm, ZB_COLS), lambda i: (i, 0)),
            pl.BlockSpec((1, tm // blk, ZV_COLS, blk),
                         lambda i: (i // tiles_per_seq, i % tiles_per_seq, 0, 0)),
            pl.BlockSpec((tm, ZC_COLS), lambda i: (i, 0)),
            pl.BlockSpec((tm // MOBA_BLOCK, 1, GROUP_WIDTH), lambda i: (i, 0, 0)),
        ],
        out_shape=[
            jax.ShapeDtypeStruct((n, 2 * GROUP_WIDTH), jnp.bfloat16),
            jax.ShapeDtypeStruct((n, ZB_COLS), jnp.bfloat16),
            jax.ShapeDtypeStruct((batch, seq // blk, ZV_COLS, blk), jnp.bfloat16),
            jax.ShapeDtypeStruct((n, ZC_COLS), jnp.float32),
            jax.ShapeDtypeStruct((n // MOBA_BLOCK, 1, GROUP_WIDTH), jnp.float32),
        ],
        scratch_shapes=[pltpu.VMEM((HALO + tm, GROUP_WIDTH), jnp.float32),
                        pltpu.VMEM((HALO + tm, GROUP_WIDTH), jnp.float32)],
        compiler_params=pltpu.CompilerParams(
            dimension_semantics=("arbitrary",), vmem_limit_bytes=VMEM_LIMIT),
    )(x2d, x2d, g, w, wt, conv_w, pool_w_bd, pool_scale)


def _conv_pool(a_in, a_c, a_b, v, halo_h, halo_v, i, cw_ref, pw_ref, ps_ref, y_ref, hbuf, vbuf):
    ts = a_in.shape[0]
    gw = GROUP_WIDTH
    hbuf[0:HALO, :] = halo_h
    vbuf[0:HALO, :] = halo_v
    hbuf[HALO:HALO + ts, :] = a_c * a_in
    vbuf[HALO:HALO + ts, :] = v

    def hist(buf, d, lo, hi):
        return buf[HALO - d:HALO - d + ts, lo:hi]

    conv = (cw_ref[0:1, :] * hist(hbuf, 2, 0, gw) + cw_ref[1:2, :] * hist(hbuf, 1, 0, gw)
            + cw_ref[2:3, :] * hist(hbuf, 0, 0, gw))
    y_ref[:, 0:gw] = (a_b * conv).astype(y_ref.dtype)

    t_pos = i * ts + lax.broadcasted_iota(jnp.int32, (ts, LANES), 0)
    lane = lax.broadcasted_iota(jnp.int32, (ts, LANES), 1)
    first_group = lane < POOL_GROUP
    halves = []
    assert 2 * POOL_GROUP == LANES and POOL_WINDOWS[-1] <= HALO
    for half, (w_small, w_big) in enumerate((POOL_WINDOWS[0:2], POOL_WINDOWS[2:4])):
        lo, hi = half * LANES, (half + 1) * LANES
        acc = hist(vbuf, 0, lo, hi)
        for d in range(1, w_small):
            acc = acc + hist(vbuf, d, lo, hi)
        s_small = acc
        for d in range(w_small, w_big):
            acc = acc + hist(vbuf, d, lo, hi)
        wsum = jnp.where(first_group, s_small, acc)
        cnt = jnp.minimum(t_pos + 1, jnp.where(first_group, w_small, w_big)).astype(jnp.float32)
        halves.append(wsum / cnt - hist(vbuf, 0, lo, hi))
    dmat = jnp.concatenate(halves, axis=1).astype(jnp.bfloat16)
    yb = _dot(dmat, pw_ref[...]) * ps_ref[...]
    y_ref[:, gw:2 * gw] = yb.astype(y_ref.dtype)


def _attn_kernel(q_ref, k_ref, vt_ref, iq_ref, ik_ref, iw_ref, kmean_ref, tdiag_ref, tprev_ref,
                 o_ref, keys_scr, planes_scr, alive_scr, mb_scr, moba_scr, qpad_scr, m_scr,
                 acc_scr, *, topk):
    qi = pl.program_id(1)
    blk = ATT_BLOCK
    hd = HEAD_DIM
    key_idx = lax.broadcasted_iota(jnp.int32, (blk, blk), 0)
    qry_idx = lax.broadcasted_iota(jnp.int32, (blk, blk), 1)
    causal = key_idx <= qry_idx

    iq = iq_ref[...]
    idx_scale = (IDX_HEADS ** -0.5) * (IDX_DIM ** -0.5)
    iw_t = iw_ref[...].T * idx_scale

    lane_q = lax.broadcasted_iota(jnp.int32, (blk, LANES), 1)

    def score_keys(j):
        ik2 = ik_ref[pl.ds(pl.multiple_of(j * blk, blk), blk), :]
        ik_half = [jnp.where(lane_q < IDX_DIM, ik2, jnp.zeros_like(ik2)),
                   jnp.where(lane_q >= IDX_DIM, ik2, jnp.zeros_like(ik2))]
        acc = jnp.zeros((blk, blk), jnp.float32)
        for h in range(IDX_HEADS):
            s = _dot_t(ik_half[h % 2], iq[:, (h // 2) * LANES:(h // 2 + 1) * LANES])
            acc = acc + jnp.maximum(s, 0.0) * iw_t[h:h + 1, :]
        return acc + 0.0

    def to_key(score):
        bits = pltpu.bitcast(score, jnp.int32)
        return jnp.where(bits < 0, bits ^ jnp.int32(0x7FFFFFFF), bits)

    def store_keys(j, key):
        keys_scr[j] = key
        words = (key ^ jnp.int32(INT_MIN)).reshape(WORD_BITS, blk // WORD_BITS, blk)
        rows = [words[v] for v in range(WORD_BITS)]
        step, mask = WORD_BITS // 2, 0x0000FFFF
        while step:
            k = 0
            while k < WORD_BITS:
                t = (rows[k] ^ lax.shift_right_logical(rows[k + step], jnp.int32(step))) & mask
                rows[k] = rows[k] ^ t
                rows[k + step] = rows[k + step] ^ lax.shift_left(t, jnp.int32(step))
                k = (k + step + 1) & ~step
            step //= 2
            mask = (mask ^ (mask << step)) & 0xFFFFFFFF if step else mask
        for bit, plane in enumerate(rows):
            planes_scr[bit, j] = plane

    def score_body(j, carry):
        store_keys(j, to_key(score_keys(j)))
        return carry

    def score_group(i, carry):
        js = [SCORE_GROUP * i + c for c in range(SCORE_GROUP)]
        scores = [score_keys(j) for j in js]
        for j, sc in zip(js, scores):
            store_keys(j, to_key(sc))
        return carry

    lax.fori_loop(0, qi // SCORE_GROUP, score_group, 0)
    lax.fori_loop(qi - qi % SCORE_GROUP, qi, score_body, 0)
    q8 = q_ref[...]
    nb = kmean_ref.shape[1]
    blk_idx = lax.broadcasted_iota(jnp.int32, (nb, blk), 0)
    blk_idx_f = blk_idx.astype(jnp.float32)
    km = kmean_ref[0]
    km_hi = km.astype(jnp.bfloat16)
    km_lo = (km - km_hi.astype(jnp.float32)).astype(jnp.bfloat16)
    for h in range(4):
        qh = q8[:, (4 + h) * hd:(5 + h) * hd]
        gate = _dot_t(km_hi[:, h * hd:(h + 1) * hd], qh) + _dot_t(km_lo[:, h * hd:(h + 1) * hd], qh)
        gate = jnp.where(blk_idx < qi, gate, -jnp.inf)
        chosen = jnp.zeros((nb, blk), jnp.bool_)
        for _ in range(MOBA_TOPB_MAX):
            mx = jnp.max(gate, axis=0, keepdims=True)
            is_mx = (gate == mx) & (gate > -jnp.inf)
            first = jnp.min(jnp.where(is_mx, blk_idx_f, float(nb)), axis=0, keepdims=True)
            pick = blk_idx_f == first
            chosen = chosen | pick
            gate = jnp.where(pick, -jnp.inf, gate)
        moba_scr[h] = jnp.where(chosen, 0.0, NEG)

    for h in range(N_HEADS):
        pair = q8[:, (h // 2) * LANES:(h // 2 + 1) * LANES]
        qpad_scr[h] = jnp.where((lane_q >= hd) == (h % 2 == 1), pair, jnp.zeros_like(pair))
    m_scr[...] = jnp.full(m_scr.shape, NEG, jnp.float32)
    acc_scr[...] = jnp.zeros(acc_scr.shape, jnp.float32)

    store_keys(qi, jnp.where(causal, to_key(score_keys(qi)), jnp.int32(INT_MIN)))

    @pl.when((pl.program_id(0) == 0) & (qi == 0))
    def _():
        planes_scr[...] = jnp.zeros(planes_scr.shape, jnp.int32)

    chunk_idx = lax.broadcasted_iota(jnp.int32, alive_scr.shape, 0)
    alive_scr[...] = jnp.where(chunk_idx <= qi, jnp.int32(-1), jnp.int32(0))

    def bit_body(it, carry):
        thr_u, need = carry
        alive = alive_scr[...]
        with_bit = alive & planes_scr[it]
        ones = jnp.sum(jnp.sum(lax.population_count(with_bit), axis=0), axis=0, keepdims=True)
        take = ones >= need
        alive_scr[...] = jnp.where(take, with_bit, alive ^ with_bit)
        thr_u = jnp.where(take, thr_u | lax.shift_left(jnp.int32(1), 31 - it), thr_u)
        return thr_u, jnp.where(take, need, need - ones)

    thr_u, need = lax.fori_loop(0, WORD_BITS, bit_body,
                                (jnp.zeros((1, blk), jnp.int32), jnp.full((1, blk), topk, jnp.int32)))
    thr = thr_u ^ jnp.int32(INT_MIN)
    n_equal = jnp.sum(jnp.sum(lax.population_count(alive_scr[...]), axis=0), axis=0, keepdims=True)
    tie_any = jnp.max((n_equal - need).astype(jnp.float32)) > 0.0

    @pl.when(jnp.logical_not(tie_any))
    def _():
        def body(j, carry):
            mb_scr[j] = jnp.where(keys_scr[j] >= thr, 0.0, NEG)
            return carry
        lax.fori_loop(0, qi, body, 0)
        mb_scr[qi] = jnp.where((keys_scr[qi] >= thr) & causal, 0.0, NEG)

    @pl.when(tie_any)
    def _():
        places = need.astype(jnp.float32)
        lower =jnp.where(qry_idx <= key_idx, 1.0, 0.0).astype(jnp.bfloat16)

        def sel_chunk(j, base):
            kj = keys_scr[j]
            eq = kj == thr
            pref = _dot(lower, jnp.where(eq, 1.0, 0.0).astype(jnp.bfloat16)) + base
            return (kj > thr) | (eq & (pref <= places)), pref[blk - 1:blk, :]

        def body(j, base):
            sel, base = sel_chunk(j, base)
            mb_scr[j] = jnp.where(sel, 0.0, NEG)
            return base
        base = lax.fori_loop(0, qi, body, jnp.zeros((1, blk), jnp.float32))
        sel, _ = sel_chunk(qi, base)
        mb_scr[qi] = jnp.where(sel & causal, 0.0, NEG)

    ones_rows = jnp.ones((ACC_ROWS - hd, blk), jnp.bfloat16)

    def tile_rows(x, rows):
        return jnp.broadcast_to(x[None], (rows // SUBLANES, SUBLANES, blk)).reshape(rows, blk)

    def logits_phase(j, kind):
        kc = k_ref[pl.ds(pl.multiple_of(j * blk, blk), blk), :]
        staged = []
        for h in range(N_HEADS):
            s = _dot_t(kc[:, (h // 2) * LANES:(h // 2 + 1) * LANES], qpad_scr[h])
            if kind == "prev":
                s = s + tprev_ref[h]
            elif kind == "diag":
                s = s + tdiag_ref[h]
            if h < 4:
                s = s + mb_scr[j]
            elif kind == "diag":
                s = jnp.where(causal, s, NEG)
            mx = jnp.max(jnp.max(s.reshape(blk // SUBLANES, SUBLANES, blk), axis=0), axis=0, keepdims=True)
            m_prev = m_scr[h]
            m_new = jnp.maximum(m_prev, mx)
            shift = m_new
            if h >= 4 and kind != "diag":
                taken = moba_scr[h - 4, pl.ds(j, 1), :] == 0.0
                m_new = jnp.where(taken, m_new, m_prev)
                shift = jnp.where(taken, m_new, -NEG)
            m_scr[h] = m_new
            staged.append((s, shift, jnp.exp2(m_prev - m_new)))
        return staged

    def exp_phase(staged):
        return [(jnp.exp2(s - tile_rows(shift, blk)).astype(jnp.bfloat16), alpha)
                for s, shift, alpha in staged]

    def output_phase(j, probs):
        for h, (p, alpha) in enumerate(probs):
            vt_ones = jnp.concatenate([vt_ref[0, j, h * hd:(h + 1) * hd, :], ones_rows], axis=0)
            acc_scr[h] = tile_rows(alpha, ACC_ROWS) * acc_scr[h] + _dot(vt_ones, p)

    def attend(chunks):
        staged = [logits_phase(j, kind) for j, kind in chunks]
        probs = [exp_phase(st) for st in staged]
        for (j, _), pr in zip(chunks, probs):
            output_phase(j, pr)

    n_far = jnp.maximum(qi - 1, 0)

    def far_group(i, carry):
        attend([(FAR_GROUP * i + c, "far") for c in range(FAR_GROUP)])
        return carry

    lax.fori_loop(0, n_far // FAR_GROUP, far_group, 0)

    def far_single(j, carry):
        attend([(j, "far")])
        return carry

    lax.fori_loop(n_far - n_far % FAR_GROUP, n_far, far_single, 0)

    @pl.when(qi >= 1)
    def _():
        attend([(qi - 1, "prev"), (qi, "diag")])

    @pl.when(qi == 0)
    def _():
        attend([(qi, "diag")])

    out_t = jnp.concatenate(
        [acc_scr[h, 0:hd, :] / acc_scr[h, hd:hd + 1, :] for h in range(N_HEADS)], axis=0)
    o_ref[...] = out_t.T.astype(o_ref.dtype)


def _attention(zb, vt, zc, kmean, tdiag, tprev, batch, seq):
    blk = ATT_BLOCK
    nq = seq // blk
    nb = kmean.shape[1]
    topk = min(DSA_TOPK_MAX, seq // 4)
    kernel = functools.partial(_attn_kernel, topk=topk)
    resident = dict(pipeline_mode=pl.Buffered(1))
    return pl.pallas_call(
        kernel,
        grid=(batch, nq),
        in_specs=[
            pl.BlockSpec((blk, W8), lambda b, i: (b * nq + i, 0)),
            pl.BlockSpec((seq, W8), lambda b, i: (b, 1)),
            pl.BlockSpec((1, nq, W8, blk), lambda b, i: (b, 0, 0, 0)),
            pl.BlockSpec((blk, W8), lambda b, i: (b * nq + i, 2)),
            pl.BlockSpec((seq, LANES), lambda b, i: (b, 3 * W8 // LANES)),
            pl.BlockSpec((blk, LANES), lambda b, i: (b * nq + i, 0)),
            pl.BlockSpec((1, nb, GROUP_WIDTH), lambda b, i: (b, 0, 0)),
            pl.BlockSpec((N_HEADS, blk, blk), lambda b, i: (0, 0, 0), **resident),
            pl.BlockSpec((N_HEADS, blk, blk), lambda b, i: (0, 0, 0), **resident),
        ],
        out_specs=pl.BlockSpec((blk, W8), lambda b, i: (b * nq + i, 0)),
        out_shape=jax.ShapeDtypeStruct((batch * seq, W8), jnp.bfloat16),
        scratch_shapes=[
            pltpu.VMEM((nq, blk, blk), jnp.int32),
            pltpu.VMEM((WORD_BITS, nq, blk // WORD_BITS, blk), jnp.int32),
            pltpu.VMEM((nq, blk // WORD_BITS, blk), jnp.int32),
            pltpu.VMEM((nq, blk, blk), jnp.float32),
            pltpu.VMEM((4, nb, blk), jnp.float32),
            pltpu.VMEM((N_HEADS, blk, LANES), jnp.bfloat16),
            pltpu.VMEM((N_HEADS, SUBLANES, blk), jnp.float32),
            pltpu.VMEM((N_HEADS, ACC_ROWS, blk), jnp.float32),
        ],
        compiler_params=pltpu.CompilerParams(
            dimension_semantics=("arbitrary", "arbitrary"), vmem_limit_bytes=VMEM_LIMIT),
    )(zb, zb, vt, zb, zb, zc, kmean, tdiag, tprev)


def _tail_kernel(x_ref, yab_ref, ycd_ref, p_ref, wo_ref, gpost_ref, gfpre_ref, wgu_ref, wd_ref,
                 gfpost_ref, gple_ref, wpg_ref, wpp_ref, o_ref, act_scr):
    tm = x_ref.shape[0]
    half = yab_ref.shape[1]
    d_ff = wd_ref.shape[0]
    subs = [slice(r * TAIL_SUB, (r + 1) * TAIL_SUB) for r in range(tm // TAIL_SUB)]

    mix = [_dot(yab_ref[r, :], wo_ref[0:half, :]) + _dot(ycd_ref[r, :], wo_ref[half:2 * half, :])
           for r in subs]
    x1 = [x_ref[r, :] + _rms(m, gpost_ref[...]) for r, m in zip(subs, mix)]
    h2 = [_rms(x, gfpre_ref[...]).astype(jnp.bfloat16) for x in x1]
    for c0 in range(0, d_ff, FFN_CHUNK):
        cw = min(FFN_CHUNK, d_ff - c0)
        gates = [_dot(h, wgu_ref[:, c0:c0 + cw]) for h in h2]
        ups = [_dot(h, wgu_ref[:, d_ff + c0:d_ff + c0 + cw]) for h in h2]
        for r, g, u in zip(subs, gates, ups):
            act_scr[r, c0:c0 + cw] = (g * jax.nn.sigmoid(g) * u).astype(jnp.bfloat16)
    f = [_dot(act_scr[r, :], wd_ref[...]) for r in subs]
    x2 = [x + _rms(y, gfpost_ref[...]) for x, y in zip(x1, f)]
    hg = [_rms(x, gple_ref[...]).astype(jnp.bfloat16) for x in x2]
    gate = [jax.nn.sigmoid(_dot(h, wpg_ref[...])) for h in hg]
    proj = [_dot(p_ref[r, :].astype(jnp.bfloat16), wpp_ref[...]) for r in subs]
    for r, x, g, pr in zip(subs, x2, gate, proj):
        o_ref[r, :] = x + g * pr


def _layer_block(stacked, layer, **kwargs):
    zeros = (0,) * (stacked.ndim - 1)
    return pl.BlockSpec((None,) + stacked.shape[1:], lambda i: (layer,) + zeros, **kwargs)


def _tail(layer, x2d, yab, ycd, p3d, wo, gpost, gfpre, wgu, wd, gfpost, gple, wpg, wpp):
    n, d = x2d.shape
    tm = ROW_TILE
    row = lambda i: (i, 0)
    resident = dict(pipeline_mode=pl.Buffered(1))
    vec = pl.BlockSpec((1, d), lambda i: (0, 0))
    return pl.pallas_call(
        _tail_kernel,
        grid=(n // tm,),
        in_specs=[
            pl.BlockSpec((tm, d), row),
            pl.BlockSpec((tm, yab.shape[1]), row),
            pl.BlockSpec((tm, ycd.shape[1]), row),
            pl.BlockSpec((None, tm, p3d.shape[2]), lambda i: (layer, i, 0)),
            _layer_block(wo, layer, **resident),
            vec, vec,
            _layer_block(wgu, layer, **resident),
            _layer_block(wd, layer, **resident),
            vec, vec,
            _layer_block(wpg, layer, **resident),
            _layer_block(wpp, layer, **resident),
        ],
        out_specs=pl.BlockSpec((tm, d), row),
        out_shape=jax.ShapeDtypeStruct((n, d), jnp.float32),
        scratch_shapes=[pltpu.VMEM((tm, wd.shape[1]), jnp.bfloat16)],
        compiler_params=pltpu.CompilerParams(
            dimension_semantics=("arbitrary",), vmem_limit_bytes=VMEM_LIMIT),
    )(x2d, yab, ycd, p3d, wo, gpost, gfpre, wgu, wd, gfpost, gple, wpg, wpp)


def _rel_bucket_np(dist):
    n = np.maximum(dist, 0)
    max_exact = REL_BUCKETS // 2
    nf = np.maximum(n, 1).astype(np.float32)
    large = max_exact + (np.log(nf / np.float32(max_exact)) / np.float32(math.log(REL_MAX_DIST / max_exact))
                         * np.float32(REL_BUCKETS - max_exact)).astype(np.int32)
    large = np.minimum(large, REL_BUCKETS - 1)
    return np.where(n < max_exact, n, large)


def _bias_tables(rel_bias, seq):
    blk = ATT_BLOCK
    key = np.arange(blk)[:, None]
    qry = np.arange(blk)[None, :]
    far = _rel_bucket_np(np.arange(blk + 1, max(seq, blk + 2)))
    assert (far == far[0]).all(), "bias must be constant beyond the previous chunk"
    tab = rel_bias.astype(jnp.float32)
    tab = (tab - tab[:, int(far[0])][:, None]) * LOG2E

    def table(bucket):
        onehot = (jnp.asarray(bucket)[None] == jnp.arange(REL_BUCKETS)[:, None, None]).astype(jnp.float32)
        return jnp.einsum("hb,bkq->hkq", tab, onehot, precision=lax.Precision.HIGHEST)

    return table(_rel_bucket_np(qry - key)), table(_rel_bucket_np(qry - key + blk))


def _split_w_in(w_in):
    rest = w_in[..., MAIN_COLS:]
    sizes = (IDX_DIM, IDX_HEADS, GROUP_WIDTH, GROUP_WIDTH, GROUP_WIDTH)
    offs = np.concatenate([[0], np.cumsum(sizes)])
    ik, iw, dq, dk, dv = (rest[..., offs[i]:offs[i + 1]] for i in range(len(sizes)))
    pad = jnp.zeros(w_in.shape[:-1] + (LANES - IDX_HEADS,), w_in.dtype)
    tail = jnp.concatenate([dq, dk, dv, ik, ik, iw, pad], axis=-1)
    return w_in[..., :MAIN_COLS].astype(jnp.bfloat16), tail.astype(jnp.bfloat16)


def _block_diag(pool_w):
    depth, ng, g, _ = pool_w.shape
    eye = jnp.eye(ng, dtype=pool_w.dtype)
    bd = jnp.einsum("lgcd,gh->lgchd", pool_w, eye).reshape(depth, ng * g, ng * g)
    return bd.astype(jnp.bfloat16)


@jax.jit
def kernel(x, p, rel_bias, g_mix_pre, w_in, conv_w, pool_w, pool_scale, w_out, g_mix_post, g_ffn_pre, w_gate_up, w_down, g_ffn_post, g_ple, w_ple_gate, w_ple_proj):
    batch, seq, d = x.shape
    depth = w_in.shape[0]
    n = batch * seq
    d_ff = w_down.shape[1]
    assert seq % ROW_TILE == 0 and d_ff % LANES == 0
    bf16 = jnp.bfloat16

    tdiag, tprev = _bias_tables(rel_bias, seq)
    w_main, w_tail = _split_w_in(w_in)
    pool_bd = _block_diag(pool_w)
    wgu = w_gate_up.astype(bf16)
    wd = w_down.astype(bf16)
    wo = w_out.astype(bf16)
    wpg = w_ple_gate.astype(bf16)
    wpp = w_ple_proj.astype(bf16)

    x2d = x.reshape(n, d)
    p3d = p.reshape(depth, n, -1)
    for i in range(depth):
        yab, zb, vt, zc, kmean = _inproj(i, x2d, g_mix_pre[i][None, :], w_main, w_tail, conv_w[i],
                                         pool_bd[i], pool_scale[i][None, :], batch, seq)
        ycd = _attention(zb, vt, zc, kmean.reshape(batch, seq // MOBA_BLOCK, GROUP_WIDTH),
                         tdiag, tprev, batch, seq)
        x2d = _tail(i, x2d, yab, ycd, p3d, wo, g_mix_post[i][None, :], g_ffn_pre[i][None, :],
                    wgu, wd, g_ffn_post[i][None, :], g_ple[i][None, :], wpg, wpp)
    return x2d.reshape(batch, seq, d)
```

```python
import functools
import math

import numpy as np
import jax
import jax.numpy as jnp
from jax import lax
from jax.experimental import pallas as pl
from jax.experimental.pallas import tpu as pltpu

HEAD_DIM = 64
GROUP_WIDTH = 256
CONV_WIDTH = 3
POOL_WINDOWS = (2, 4, 8, 16)
POOL_GROUP = GROUP_WIDTH // len(POOL_WINDOWS)
IDX_HEADS = 8
IDX_DIM = 64
DSA_TOPK_MAX = 256
MOBA_BLOCK = 256
MOBA_TOPB_MAX = 3
REL_BUCKETS = 32
REL_MAX_DIST = 128
N_HEADS = 8
RMS_EPS = 1e-6

LANES = 128
SUBLANES = 8
ATT_BLOCK = 256
FAR_GROUP = 4
SCORE_GROUP = 4
HALO = 16
NEG = -1e30
LOG2E = math.log2(math.e)
ACC_ROWS = HEAD_DIM + 16
INT_MIN = -(2 ** 31)
WORD_BITS = 32
VMEM_LIMIT = 56 * 1024 * 1024
ROW_TILE = 512
TAIL_SUB = 256
FFN_CHUNK = 1024

gw_ = GROUP_WIDTH
SRC = dict(a_in=0, a_c=gw_, a_b=2 * gw_, pool_v=3 * gw_, cq=4 * gw_, ck=5 * gw_, cv=6 * gw_, iq=7 * gw_)
MAIN_COLS = 7 * gw_ + IDX_HEADS * IDX_DIM
TAIL = dict(dq=0, dk=gw_, dv=2 * gw_, ik2=3 * gw_, iw=3 * gw_ + LANES)
TAIL_COLS = 3 * gw_ + 2 * LANES
W8 = N_HEADS * HEAD_DIM
ZB_COLS = 2 * W8 + IDX_HEADS * IDX_DIM + 2 * IDX_DIM
ZV_COLS = W8
ZC_COLS = LANES
QSCALE = LOG2E * HEAD_DIM ** -0.5


def _rms(x, g):
    return x * lax.rsqrt(jnp.mean(x * x, axis=-1, keepdims=True) + RMS_EPS) * g


def _dot(a, b):
    return jnp.dot(a, b, preferred_element_type=jnp.float32)


def _dot_t(a, b):
    return lax.dot_general(a, b, (((1,), (1,)), ((), ())), preferred_element_type=jnp.float32)


def _inproj_kernel(x_ref, xprev_ref, g_ref, w_ref, wt_ref, cw_ref, pw_ref, ps_ref, yab_ref, zb_ref,
                   vt_ref, zc_ref, kmean_ref, hbuf, vbuf, *, tiles_per_seq):
    tm = x_ref.shape[0]
    blk = ATT_BLOCK
    gw = GROUP_WIDTH
    h = _rms(x_ref[...], g_ref[...]).astype(jnp.bfloat16)
    seq_tile = pl.program_id(0) % tiles_per_seq
    hp = _rms(xprev_ref[...], g_ref[...]).astype(jnp.bfloat16)
    prev = [_dot(hp, w_ref[:, SRC[name]:SRC[name] + gw]) for name in ("a_in", "a_c", "pool_v")]
    halo_h = jnp.where(seq_tile == 0, 0.0, prev[1] * prev[0])
    halo_v = jnp.where(seq_tile == 0, 0.0, prev[2])

    def main(name, off=0, width=gw):
        c0 = SRC[name] + off
        return _dot(h, w_ref[:, c0:c0 + width])

    def tail(name, width=gw):
        return _dot(h, wt_ref[:, TAIL[name]:TAIL[name] + width])

    mixer_in = [main("a_in"), main("a_c"), main("a_b"), main("pool_v")]
    zb_ref[:, 0:gw] = (main("cq") * QSCALE).astype(jnp.bfloat16)
    zb_ref[:, gw:2 * gw] = (tail("dq") * QSCALE).astype(jnp.bfloat16)
    zb_ref[:, 2 * gw:3 * gw] = main("ck").astype(jnp.bfloat16)
    dk = tail("dk")
    zb_ref[:, 3 * gw:4 * gw] = dk.astype(jnp.bfloat16)
    for r in range(tm // MOBA_BLOCK):
        kmean_ref[r] = jnp.mean(dk[r * MOBA_BLOCK:(r + 1) * MOBA_BLOCK], axis=0, keepdims=True)
    for c0 in range(0, IDX_HEADS * IDX_DIM, gw):
        zb_ref[:, 2 * W8 + c0:2 * W8 + c0 + gw] = main("iq", c0).astype(jnp.bfloat16)
    zb_ref[:, ZB_COLS - LANES:ZB_COLS] = tail("ik2", LANES).astype(jnp.bfloat16)
    for c0, z in ((0, main("cv")), (gw, tail("dv"))):
        for r in range(tm // blk):
            vt_ref[0, r, c0:c0 + gw, :] = z[r * blk:(r + 1) * blk, :].T.astype(jnp.bfloat16)
    zc_ref[...] = tail("iw", LANES)
    _conv_pool(*mixer_in, halo_h, halo_v, seq_tile, cw_ref, pw_ref, ps_ref, yab_ref, hbuf, vbuf)


def _inproj(layer, x2d, g, w, wt, conv_w, pool_w_bd, pool_scale, batch, seq):
    n, d = x2d.shape
    tm = ROW_TILE
    blk = ATT_BLOCK
    tiles_per_seq = seq // tm
    const = lambda i: (0, 0)
    return pl.pallas_call(
        functools.partial(_inproj_kernel, tiles_per_seq=tiles_per_seq),
        grid=(n // tm,),
        in_specs=[
            pl.BlockSpec((tm, d), lambda i: (i, 0)),
            pl.BlockSpec((HALO, d), lambda i: (jnp.maximum(i * (tm // HALO) - 1, 0), 0)),
            pl.BlockSpec((1, d), const),
            _layer_block(w, layer, pipeline_mode=pl.Buffered(1)),
            _layer_block(wt, layer, pipeline_mode=pl.Buffered(1)),
            pl.BlockSpec((CONV_WIDTH, GROUP_WIDTH), const),
            pl.BlockSpec((GROUP_WIDTH, GROUP_WIDTH), const),
            pl.BlockSpec((1, GROUP_WIDTH), const),
        ],
        out_specs=[
            pl.BlockSpec((tm, 2 * GROUP_WIDTH), lambda i: (i, 0)),
            pl.BlockSpec((tm, ZB_COLS), lambda i: (i, 0)),
            pl.BlockSpec((1, tm // blk, ZV_COLS, blk),
                         lambda i: (i // tiles_per_seq, i % tiles_per_seq, 0, 0)),
            pl.BlockSpec((tm, ZC_COLS), lambda i: (i, 0)),
            pl.BlockSpec((tm // MOBA_BLOCK, 1, GROUP_WIDTH), lambda i: (i, 0, 0)),
        ],
        out_shape=[
            jax.ShapeDtypeStruct((n, 2 * GROUP_WIDTH), jnp.bfloat16),
            jax.ShapeDtypeStruct((n, ZB_COLS), jnp.bfloat16),
            jax.ShapeDtypeStruct((batch, seq // blk, ZV_COLS, blk), jnp.bfloat16),
            jax.ShapeDtypeStruct((n, ZC_COLS), jnp.float32),
            jax.ShapeDtypeStruct((n // MOBA_BLOCK, 1, GROUP_WIDTH), jnp.float32),
        ],
        scratch_shapes=[pltpu.VMEM((HALO + tm, GROUP_WIDTH), jnp.float32),
                        pltpu.VMEM((HALO + tm, GROUP_WIDTH), jnp.float32)],
        compiler_params=pltpu.CompilerParams(
            dimension_semantics=("arbitrary",), vmem_limit_bytes=VMEM_LIMIT),
    )(x2d, x2d, g, w, wt, conv_w, pool_w_bd, pool_scale)


def _conv_pool(a_in, a_c, a_b, v, halo_h, halo_v, i, cw_ref, pw_ref, ps_ref, y_ref, hbuf, vbuf):
    ts = a_in.shape[0]
    gw = GROUP_WIDTH
    hbuf[0:HALO, :] = halo_h
    vbuf[0:HALO, :] = halo_v
    hbuf[HALO:HALO + ts, :] = a_c * a_in
    vbuf[HALO:HALO + ts, :] = v

    def hist(buf, d, lo, hi):
        return buf[HALO - d:HALO - d + ts, lo:hi]

    conv = (cw_ref[0:1, :] * hist(hbuf, 2, 0, gw) + cw_ref[1:2, :] * hist(hbuf, 1, 0, gw)
            + cw_ref[2:3, :] * hist(hbuf, 0, 0, gw))
    y_ref[:, 0:gw] = (a_b * conv).astype(y_ref.dtype)

    t_pos = i * ts + lax.broadcasted_iota(jnp.int32, (ts, LANES), 0)
    lane = lax.broadcasted_iota(jnp.int32, (ts, LANES), 1)
    first_group = lane < POOL_GROUP
    halves = []
    assert 2 * POOL_GROUP == LANES and POOL_WINDOWS[-1] <= HALO
    for half, (w_small, w_big) in enumerate((POOL_WINDOWS[0:2], POOL_WINDOWS[2:4])):
        lo, hi = half * LANES, (half + 1) * LANES
        acc = hist(vbuf, 0, lo, hi)
        for d in range(1, w_small):
            acc = acc + hist(vbuf, d, lo, hi)
        s_small = acc
        for d in range(w_small, w_big):
            acc = acc + hist(vbuf, d, lo, hi)
        wsum = jnp.where(first_group, s_small, acc)
        cnt = jnp.minimum(t_pos + 1, jnp.where(first_group, w_small, w_big)).astype(jnp.float32)
        halves.append(wsum / cnt - hist(vbuf, 0, lo, hi))
    dmat = jnp.concatenate(halves, axis=1).astype(jnp.bfloat16)
    yb = _dot(dmat, pw_ref[...]) * ps_ref[...]
    y_ref[:, gw:2 * gw] = yb.astype(y_ref.dtype)


def _attn_kernel(q_ref, k_ref, vt_ref, iq_ref, ik_ref, iw_ref, kmean_ref, tdiag_ref, tprev_ref,
                 o_ref, keys_scr, planes_scr, alive_scr, mb_scr, moba_scr, qpad_scr, m_scr,
                 acc_scr, *, topk):
    qi = pl.program_id(1)
    blk = ATT_BLOCK
    hd = HEAD_DIM
    key_idx = lax.broadcasted_iota(jnp.int32, (blk, blk), 0)
    qry_idx = lax.broadcasted_iota(jnp.int32, (blk, blk), 1)
    causal = key_idx <= qry_idx

    iq = iq_ref[...]
    idx_scale = (IDX_HEADS ** -0.5) * (IDX_DIM ** -0.5)
    iw_t = iw_ref[...].T * idx_scale

    lane_q = lax.broadcasted_iota(jnp.int32, (blk, LANES), 1)

    def score_keys(j):
        ik2 = ik_ref[pl.ds(pl.multiple_of(j * blk, blk), blk), :]
        ik_half = [jnp.where(lane_q < IDX_DIM, ik2, jnp.zeros_like(ik2)),
                   jnp.where(lane_q >= IDX_DIM, ik2, jnp.zeros_like(ik2))]
        acc = jnp.zeros((blk, blk), jnp.float32)
        for h in range(IDX_HEADS):
            s = _dot_t(ik_half[h % 2], iq[:, (h // 2) * LANES:(h // 2 + 1) * LANES])
            acc = acc + jnp.maximum(s, 0.0) * iw_t[h:h + 1, :]
        return acc + 0.0

    def to_key(score):
        bits = pltpu.bitcast(score, jnp.int32)
        return jnp.where(bits < 0, bits ^ jnp.int32(0x7FFFFFFF), bits)

    def store_keys(j, key):
        keys_scr[j] = key
        words = (key ^ jnp.int32(INT_MIN)).reshape(WORD_BITS, blk // WORD_BITS, blk)
        rows = [words[v] for v in range(WORD_BITS)]
        step, mask = WORD_BITS // 2, 0x0000FFFF
        while step:
            k = 0
            while k < WORD_BITS:
                t = (rows[k] ^ lax.shift_right_logical(rows[k + step], jnp.int32(step))) & mask
                rows[k] = rows[k] ^ t
                rows[k + step] = rows[k + step] ^ lax.shift_left(t, jnp.int32(step))
                k = (k + step + 1) & ~step
            step //= 2
            mask = (mask ^ (mask << step)) & 0xFFFFFFFF if step else mask
        for bit, plane in enumerate(rows):
            planes_scr[bit, j] = plane

    def score_body(j, carry):
        store_keys(j, to_key(score_keys(j)))
        return carry

    def score_group(i, carry):
        js = [SCORE_GROUP * i + c for c in range(SCORE_GROUP)]
        scores = [score_keys(j) for j in js]
        for j, sc in zip(js, scores):
            store_keys(j, to_key(sc))
        return carry

    lax.fori_loop(0, qi // SCORE_GROUP, score_group, 0)
    lax.fori_loop(qi - qi % SCORE_GROUP, qi, score_body, 0)
    q8 = q_ref[...]
    nb = kmean_ref.shape[1]
    blk_idx = lax.broadcasted_iota(jnp.int32, (nb, blk), 0)
    blk_idx_f = blk_idx.astype(jnp.float32)
    km = kmean_ref[0]
    km_hi = km.astype(jnp.bfloat16)
    km_lo = (km - km_hi.astype(jnp.float32)).astype(jnp.bfloat16)
    for h in range(4):
        qh = q8[:, (4 + h) * hd:(5 + h) * hd]
        gate = _dot_t(km_hi[:, h * hd:(h + 1) * hd], qh) + _dot_t(km_lo[:, h * hd:(h + 1) * hd], qh)
        gate = jnp.where(blk_idx < qi, gate, -jnp.inf)
        chosen = jnp.zeros((nb, blk), jnp.bool_)
        for _ in range(MOBA_TOPB_MAX):
            mx = jnp.max(gate, axis=0, keepdims=True)
            is_mx = (gate == mx) & (gate > -jnp.inf)
            first = jnp.min(jnp.where(is_mx, blk_idx_f, float(nb)), axis=0, keepdims=True)
            pick = blk_idx_f == first
            chosen = chosen | pick
            gate = jnp.where(pick, -jnp.inf, gate)
        moba_scr[h] = jnp.where(chosen, 0.0, NEG)

    for h in range(N_HEADS):
        pair = q8[:, (h // 2) * LANES:(h // 2 + 1) * LANES]
        qpad_scr[h] = jnp.where((lane_q >= hd) == (h % 2 == 1), pair, jnp.zeros_like(pair))
    m_scr[...] = jnp.full(m_scr.shape, NEG, jnp.float32)
    acc_scr[...] = jnp.zeros(acc_scr.shape, jnp.float32)

    store_keys(qi, jnp.where(causal, to_key(score_keys(qi)), jnp.int32(INT_MIN)))

    @pl.when((pl.program_id(0) == 0) & (qi == 0))
    def _():
        planes_scr[...] = jnp.zeros(planes_scr.shape, jnp.int32)

    chunk_idx = lax.broadcasted_iota(jnp.int32, alive_scr.shape, 0)
    alive_scr[...] = jnp.where(chunk_idx <= qi, jnp.int32(-1), jnp.int32(0))

    def bit_body(it, carry):
        thr_u, need = carry
        alive = alive_scr[...]
        with_bit = alive & planes_scr[it]
        ones = jnp.sum(jnp.sum(lax.population_count(with_bit), axis=0), axis=0, keepdims=True)
        take = ones >= need
        alive_scr[...] = jnp.where(take, with_bit, alive ^ with_bit)
        thr_u = jnp.where(take, thr_u | lax.shift_left(jnp.int32(1), 31 - it), thr_u)
        return thr_u, jnp.where(take, need, need - ones)

    thr_u, need = lax.fori_loop(0, WORD_BITS, bit_body,
                                (jnp.zeros((1, blk), jnp.int32), jnp.full((1, blk), topk, jnp.int32)))
    thr = thr_u ^ jnp.int32(INT_MIN)
    n_equal = jnp.sum(jnp.sum(lax.population_count(alive_scr[...]), axis=0), axis=0, keepdims=True)
    tie_any = jnp.max((n_equal - need).astype(jnp.float32)) > 0.0

    @pl.when(jnp.logical_not(tie_any))
    def _():
        def body(j, carry):
            mb_scr[j] = jnp.where(keys_scr[j] >= thr, 0.0, NEG)
            return carry
        lax.fori_loop(0, qi, body, 0)
        mb_scr[qi] = jnp.where((keys_scr[qi] >= thr) & causal, 0.0, NEG)

    @pl.when(tie_any)
    def _():
        places = need.astype(jnp.float32)
        lower =jnp.where(qry_idx <= key_idx, 1.0, 0.0).astype(jnp.bfloat16)

        def sel_chunk(j, base):
            kj = keys_scr[j]
            eq = kj == thr
            pref = _dot(lower, jnp.where(eq, 1.0, 0.0).astype(jnp.bfloat16)) + base
            return (kj > thr) | (eq & (pref <= places)), pref[blk - 1:blk, :]

        def body(j, base):
            sel, base = sel_chunk(j, base)
            mb_scr[j] = jnp.where(sel, 0.0, NEG)
            return base
        base = lax.fori_loop(0, qi, body, jnp.zeros((1, blk), jnp.float32))
        sel, _ = sel_chunk(qi, base)
        mb_scr[qi] = jnp.where(sel & causal, 0.0, NEG)

    ones_rows = jnp.ones((ACC_ROWS - hd, blk), jnp.bfloat16)

    def tile_rows(x, rows):
        return jnp.broadcast_to(x[None], (rows // SUBLANES, SUBLANES, blk)).reshape(rows, blk)

    def logits_phase(j, kind):
        kc = k_ref[pl.ds(pl.multiple_of(j * blk, blk), blk), :]
        staged = []
        for h in range(N_HEADS):
            s = _dot_t(kc[:, (h // 2) * LANES:(h // 2 + 1) * LANES], qpad_scr[h])
            if kind == "prev":
                s = s + tprev_ref[h]
            elif kind == "diag":
                s = s + tdiag_ref[h]
            if h < 4:
                s = s + mb_scr[j]
            elif kind == "diag":
                s = jnp.where(causal, s, NEG)
            mx = jnp.max(jnp.max(s.reshape(blk // SUBLANES, SUBLANES, blk), axis=0), axis=0, keepdims=True)
            m_prev = m_scr[h]
            m_new = jnp.maximum(m_prev, mx)
            shift = m_new
            if h >= 4 and kind != "diag":
                taken = moba_scr[h - 4, pl.ds(j, 1), :] == 0.0
                m_new = jnp.where(taken, m_new, m_prev)
                shift = jnp.where(taken, m_new, -NEG)
            m_scr[h] = m_new
            staged.append((s, shift, jnp.exp2(m_prev - m_new)))
        return staged

    def exp_phase(staged):
        return [(jnp.exp2(s - tile_rows(shift, blk)).astype(jnp.bfloat16), alpha)
                for s, shift, alpha in staged]

    def output_phase(j, probs):
        for h, (p, alpha) in enumerate(probs):
            vt_ones = jnp.concatenate([vt_ref[0, j, h * hd:(h + 1) * hd, :], ones_rows], axis=0)
            acc_scr[h] = tile_rows(alpha, ACC_ROWS) * acc_scr[h] + _dot(vt_ones, p)

    def attend(chunks):
        staged = [logits_phase(j, kind) for j, kind in chunks]
        probs = [exp_phase(st) for st in staged]
        for (j, _), pr in zip(chunks, probs):
            output_phase(j, pr)

    n_far = jnp.maximum(qi - 1, 0)

    def far_group(i, carry):
        attend([(FAR_GROUP * i + c, "far") for c in range(FAR_GROUP)])
        return carry

    lax.fori_loop(0, n_far // FAR_GROUP, far_group, 0)

    left = n_far % FAR_GROUP
    for r in range(FAR_GROUP):
        @pl.when((qi >= 1) & (left == r))
        def _(r=r):
            attend([(n_far - r + c, "far") for c in range(r)] + [(qi - 1, "prev"), (qi, "diag")])

    @pl.when(qi == 0)
    def _():
        attend([(qi, "diag")])

    out_t = jnp.concatenate(
        [acc_scr[h, 0:hd, :] / acc_scr[h, hd:hd + 1, :] for h in range(N_HEADS)], axis=0)
    o_ref[...] = out_t.T.astype(o_ref.dtype)


def _attention(zb, vt, zc, kmean, tdiag, tprev, batch, seq):
    blk = ATT_BLOCK
    nq = seq // blk
    nb = kmean.shape[1]
    topk = min(DSA_TOPK_MAX, seq // 4)
    kernel = functools.partial(_attn_kernel, topk=topk)
    resident = dict(pipeline_mode=pl.Buffered(1))
    return pl.pallas_call(
        kernel,
        grid=(batch, nq),
        in_specs=[
            pl.BlockSpec((blk, W8), lambda b, i: (b * nq + i, 0)),
            pl.BlockSpec((seq, W8), lambda b, i: (b, 1)),
            pl.BlockSpec((1, nq, W8, blk), lambda b, i: (b, 0, 0, 0)),
            pl.BlockSpec((blk, W8), lambda b, i: (b * nq + i, 2)),
            pl.BlockSpec((seq, LANES), lambda b, i: (b, 3 * W8 // LANES)),
            pl.BlockSpec((blk, LANES), lambda b, i: (b * nq + i, 0)),
            pl.BlockSpec((1, nb, GROUP_WIDTH), lambda b, i: (b, 0, 0)),
            pl.BlockSpec((N_HEADS, blk, blk), lambda b, i: (0, 0, 0), **resident),
            pl.BlockSpec((N_HEADS, blk, blk), lambda b, i: (0, 0, 0), **resident),
        ],
        out_specs=pl.BlockSpec((blk, W8), lambda b, i: (b * nq + i, 0)),
        out_shape=jax.ShapeDtypeStruct((batch * seq, W8), jnp.bfloat16),
        scratch_shapes=[
            pltpu.VMEM((nq, blk, blk), jnp.int32),
            pltpu.VMEM((WORD_BITS, nq, blk // WORD_BITS, blk), jnp.int32),
            pltpu.VMEM((nq, blk // WORD_BITS, blk), jnp.int32),
            pltpu.VMEM((nq, blk, blk), jnp.float32),
            pltpu.VMEM((4, nb, blk), jnp.float32),
            pltpu.VMEM((N_HEADS, blk, LANES), jnp.bfloat16),
            pltpu.VMEM((N_HEADS, SUBLANES, blk), jnp.float32),
            pltpu.VMEM((N_HEADS, ACC_ROWS, blk), jnp.float32),
        ],
        compiler_params=pltpu.CompilerParams(
            dimension_semantics=("arbitrary", "arbitrary"), vmem_limit_bytes=VMEM_LIMIT),
    )(zb, zb, vt, zb, zb, zc, kmean, tdiag, tprev)


def _tail_kernel(x_ref, yab_ref, ycd_ref, p_ref, wo_ref, gpost_ref, gfpre_ref, wgu_ref, wd_ref,
                 gfpost_ref, gple_ref, wpg_ref, wpp_ref, o_ref, act_scr):
    tm = x_ref.shape[0]
    half = yab_ref.shape[1]
    d_ff = wd_ref.shape[0]
    subs = [slice(r * TAIL_SUB, (r + 1) * TAIL_SUB) for r in range(tm // TAIL_SUB)]

    mix = [_dot(yab_ref[r, :], wo_ref[0:half, :]) + _dot(ycd_ref[r, :], wo_ref[half:2 * half, :])
           for r in subs]
    x1 = [x_ref[r, :] + _rms(m, gpost_ref[...]) for r, m in zip(subs, mix)]
    h2 = [_rms(x, gfpre_ref[...]).astype(jnp.bfloat16) for x in x1]
    for c0 in range(0, d_ff, FFN_CHUNK):
        cw = min(FFN_CHUNK, d_ff - c0)
        gates = [_dot(h, wgu_ref[:, c0:c0 + cw]) for h in h2]
        ups = [_dot(h, wgu_ref[:, d_ff + c0:d_ff + c0 + cw]) for h in h2]
        for r, g, u in zip(subs, gates, ups):
            act_scr[r, c0:c0 + cw] = (g * jax.nn.sigmoid(g) * u).astype(jnp.bfloat16)
    f = [_dot(act_scr[r, :], wd_ref[...]) for r in subs]
    x2 = [x + _rms(y, gfpost_ref[...]) for x, y in zip(x1, f)]
    hg = [_rms(x, gple_ref[...]).astype(jnp.bfloat16) for x in x2]
    gate = [jax.nn.sigmoid(_dot(h, wpg_ref[...])) for h in hg]
    proj = [_dot(p_ref[r, :].astype(jnp.bfloat16), wpp_ref[...]) for r in subs]
    for r, x, g, pr in zip(subs, x2, gate, proj):
        o_ref[r, :] = x + g * pr


def _layer_block(stacked, layer, **kwargs):
    zeros = (0,) * (stacked.ndim - 1)
    return pl.BlockSpec((None,) + stacked.shape[1:], lambda i: (layer,) + zeros, **kwargs)


def _tail(layer, x2d, yab, ycd, p3d, wo, gpost, gfpre, wgu, wd, gfpost, gple, wpg, wpp):
    n, d = x2d.shape
    tm = ROW_TILE
    row = lambda i: (i, 0)
    resident = dict(pipeline_mode=pl.Buffered(1))
    vec = pl.BlockSpec((1, d), lambda i: (0, 0))
    return pl.pallas_call(
        _tail_kernel,
        grid=(n // tm,),
        in_specs=[
            pl.BlockSpec((tm, d), row),
            pl.BlockSpec((tm, yab.shape[1]), row),
            pl.BlockSpec((tm, ycd.shape[1]), row),
            pl.BlockSpec((None, tm, p3d.shape[2]), lambda i: (layer, i, 0)),
            _layer_block(wo, layer, **resident),
            vec, vec,
            _layer_block(wgu, layer, **resident),
            _layer_block(wd, layer, **resident),
            vec, vec,
            _layer_block(wpg, layer, **resident),
            _layer_block(wpp, layer, **resident),
        ],
        out_specs=pl.BlockSpec((tm, d), row),
        out_shape=jax.ShapeDtypeStruct((n, d), jnp.float32),
        scratch_shapes=[pltpu.VMEM((tm, wd.shape[1]), jnp.bfloat16)],
        compiler_params=pltpu.CompilerParams(
            dimension_semantics=("arbitrary",), vmem_limit_bytes=VMEM_LIMIT),
    )(x2d, yab, ycd, p3d, wo, gpost, gfpre, wgu, wd, gfpost, gple, wpg, wpp)


def _rel_bucket_np(dist):
    n = np.maximum(dist, 0)
    max_exact = REL_BUCKETS // 2
    nf = np.maximum(n, 1).astype(np.float32)
    large = max_exact + (np.log(nf / np.float32(max_exact)) / np.float32(math.log(REL_MAX_DIST / max_exact))
                         * np.float32(REL_BUCKETS - max_exact)).astype(np.int32)
    large = np.minimum(large, REL_BUCKETS - 1)
    return np.where(n < max_exact, n, large)


def _bias_tables(rel_bias, seq):
    blk = ATT_BLOCK
    key = np.arange(blk)[:, None]
    qry = np.arange(blk)[None, :]
    far = _rel_bucket_np(np.arange(blk + 1, max(seq, blk + 2)))
    assert (far == far[0]).all(), "bias must be constant beyond the previous chunk"
    tab = rel_bias.astype(jnp.float32)
    tab = (tab - tab[:, int(far[0])][:, None]) * LOG2E

    def table(bucket):
        onehot = (jnp.asarray(bucket)[None] == jnp.arange(REL_BUCKETS)[:, None, None]).astype(jnp.float32)
        return jnp.einsum("hb,bkq->hkq", tab, onehot, precision=lax.Precision.HIGHEST)

    return table(_rel_bucket_np(qry - key)), table(_rel_bucket_np(qry - key + blk))


def _split_w_in(w_in):
    rest = w_in[..., MAIN_COLS:]
    sizes = (IDX_DIM, IDX_HEADS, GROUP_WIDTH, GROUP_WIDTH, GROUP_WIDTH)
    offs = np.concatenate([[0], np.cumsum(sizes)])
    ik, iw, dq, dk, dv = (rest[..., offs[i]:offs[i + 1]] for i in range(len(sizes)))
    pad = jnp.zeros(w_in.shape[:-1] + (LANES - IDX_HEADS,), w_in.dtype)
    tail = jnp.concatenate([dq, dk, dv, ik, ik, iw, pad], axis=-1)
    return w_in[..., :MAIN_COLS].astype(jnp.bfloat16), tail.astype(jnp.bfloat16)


def _block_diag(pool_w):
    depth, ng, g, _ = pool_w.shape
    eye = jnp.eye(ng, dtype=pool_w.dtype)
    bd = jnp.einsum("lgcd,gh->lgchd", pool_w, eye).reshape(depth, ng * g, ng * g)
    return bd.astype(jnp.bfloat16)


@jax.jit
def kernel(x, p, rel_bias, g_mix_pre, w_in, conv_w, pool_w, pool_scale, w_out, g_mix_post, g_ffn_pre, w_gate_up, w_down, g_ffn_post, g_ple, w_ple_gate, w_ple_proj):
    batch, seq, d = x.shape
    depth = w_in.shape[0]
    n = batch * seq
    d_ff = w_down.shape[1]
    assert seq % ROW_TILE == 0 and d_ff % LANES == 0
    bf16 = jnp.bfloat16

    tdiag, tprev = _bias_tables(rel_bias, seq)
    w_main, w_tail = _split_w_in(w_in)
    pool_bd = _block_diag(pool_w)
    wgu = w_gate_up.astype(bf16)
    wd = w_down.astype(bf16)
    wo = w_out.astype(bf16)
    wpg = w_ple_gate.astype(bf16)
    wpp = w_ple_proj.astype(bf16)

    x2d = x.reshape(n, d)
    p3d = p.reshape(depth, n, -1)
    for i in range(depth):
        yab, zb, vt, zc, kmean = _inproj(i, x2d, g_mix_pre[i][None, :], w_main, w_tail, conv_w[i],
                                         pool_bd[i], pool_scale[i][None, :], batch, seq)
        ycd = _attention(zb, vt, zc, kmean.reshape(batch, seq // MOBA_BLOCK, GROUP_WIDTH),
                         tdiag, tprev, batch, seq)
        x2d = _tail(i, x2d, yab, ycd, p3d, wo, g_mix_post[i][None, :], g_ffn_pre[i][None, :],
                    wgu, wd, g_ffn_post[i][None, :], g_ple[i][None, :], wpg, wpp)
    return x2d.reshape(batch, seq, d)
```

```python
import functools
import math

import numpy as np
import jax
import jax.numpy as jnp
from jax import lax
from jax.experimental import pallas as pl
from jax.experimental.pallas import tpu as pltpu

HEAD_DIM = 64
GROUP_WIDTH = 256
CONV_WIDTH = 3
POOL_WINDOWS = (2, 4, 8, 16)
POOL_GROUP = GROUP_WIDTH // len(POOL_WINDOWS)
IDX_HEADS = 8
IDX_DIM = 64
DSA_TOPK_MAX = 256
MOBA_BLOCK = 256
MOBA_TOPB_MAX = 3
REL_BUCKETS = 32
REL_MAX_DIST = 128
N_HEADS = 8
RMS_EPS = 1e-6

LANES = 128
SUBLANES = 8
ATT_BLOCK = 256
FAR_GROUP = 4
SCORE_GROUP = 4
HALO = 16
NEG = -1e30
LOG2E = math.log2(math.e)
ACC_ROWS = HEAD_DIM + 16
INT_MIN = -(2 ** 31)
WORD_BITS = 32
VMEM_LIMIT = 56 * 1024 * 1024
ROW_TILE = 512
TAIL_SUB = 256
FFN_CHUNK = 1024

gw_ = GROUP_WIDTH
SRC = dict(a_in=0, a_c=gw_, a_b=2 * gw_, pool_v=3 * gw_, cq=4 * gw_, ck=5 * gw_, cv=6 * gw_, iq=7 * gw_)
MAIN_COLS = 7 * gw_ + IDX_HEADS * IDX_DIM
TAIL = dict(dq=0, dk=gw_, dv=2 * gw_, ik2=3 * gw_, iw=3 * gw_ + LANES)
TAIL_COLS = 3 * gw_ + 2 * LANES
W8 = N_HEADS * HEAD_DIM
ZB_COLS = 2 * W8 + IDX_HEADS * IDX_DIM + 2 * IDX_DIM
ZV_COLS = W8
ZC_COLS = LANES
QSCALE = LOG2E * HEAD_DIM ** -0.5


def _rms(x, g):
    return x * lax.rsqrt(jnp.mean(x * x, axis=-1, keepdims=True) + RMS_EPS) * g


def _dot(a, b):
    return jnp.dot(a, b, preferred_element_type=jnp.float32)


def _dot_t(a, b):
    return lax.dot_general(a, b, (((1,), (1,)), ((), ())), preferred_element_type=jnp.float32)


def _inproj_kernel(x_ref, xprev_ref, g_ref, w_ref, wt_ref, cw_ref, pw_ref, ps_ref, yab_ref, zb_ref,
                   vt_ref, zc_ref, kmean_ref, hbuf, vbuf, *, tiles_per_seq):
    tm = x_ref.shape[0]
    blk = ATT_BLOCK
    gw = GROUP_WIDTH
    h = _rms(x_ref[...], g_ref[...]).astype(jnp.bfloat16)
    seq_tile = pl.program_id(0) % tiles_per_seq
    hp = _rms(xprev_ref[...], g_ref[...]).astype(jnp.bfloat16)
    prev = [_dot(hp, w_ref[:, SRC[name]:SRC[name] + gw]) for name in ("a_in", "a_c", "pool_v")]
    halo_h = jnp.where(seq_tile == 0, 0.0, prev[1] * prev[0])
    halo_v = jnp.where(seq_tile == 0, 0.0, prev[2])

    def main(name, off=0, width=gw):
        c0 = SRC[name] + off
        return _dot(h, w_ref[:, c0:c0 + width])

    def tail(name, width=gw):
        return _dot(h, wt_ref[:, TAIL[name]:TAIL[name] + width])

    mixer_in = [main("a_in"), main("a_c"), main("a_b"), main("pool_v")]
    zb_ref[:, 0:gw] = (main("cq") * QSCALE).astype(jnp.bfloat16)
    zb_ref[:, gw:2 * gw] = (tail("dq") * QSCALE).astype(jnp.bfloat16)
    zb_ref[:, 2 * gw:3 * gw] = main("ck").astype(jnp.bfloat16)
    dk = tail("dk")
    zb_ref[:, 3 * gw:4 * gw] = dk.astype(jnp.bfloat16)
    for r in range(tm // MOBA_BLOCK):
        kmean_ref[r] = jnp.mean(dk[r * MOBA_BLOCK:(r + 1) * MOBA_BLOCK], axis=0, keepdims=True)
    for c0 in range(0, IDX_HEADS * IDX_DIM, gw):
        zb_ref[:, 2 * W8 + c0:2 * W8 + c0 + gw] = main("iq", c0).astype(jnp.bfloat16)
    zb_ref[:, ZB_COLS - LANES:ZB_COLS] = tail("ik2", LANES).astype(jnp.bfloat16)
    for c0, z in ((0, main("cv")), (gw, tail("dv"))):
        for r in range(tm // blk):
            vt_ref[0, r, c0:c0 + gw, :] = z[r * blk:(r + 1) * blk, :].T.astype(jnp.bfloat16)
    zc_ref[...] = tail("iw", LANES)
    _conv_pool(*mixer_in, halo_h, halo_v, seq_tile, cw_ref, pw_ref, ps_ref, yab_ref, hbuf, vbuf)


def _inproj(layer, x2d, g, w, wt, conv_w, pool_w_bd, pool_scale, batch, seq):
    n, d = x2d.shape
    tm = ROW_TILE
    blk = ATT_BLOCK
    tiles_per_seq = seq // tm
    const = lambda i: (0, 0)
    return pl.pallas_call(
        functools.partial(_inproj_kernel, tiles_per_seq=tiles_per_seq),
        grid=(n // tm,),
        in_specs=[
            pl.BlockSpec((tm, d), lambda i: (i, 0)),
            pl.BlockSpec((HALO, d), lambda i: (jnp.maximum(i * (tm // HALO) - 1, 0), 0)),
            pl.BlockSpec((1, d), const),
            _layer_block(w, layer, pipeline_mode=pl.Buffered(1)),
            _layer_block(wt, layer, pipeline_mode=pl.Buffered(1)),
            pl.BlockSpec((CONV_WIDTH, GROUP_WIDTH), const),
            pl.BlockSpec((GROUP_WIDTH, GROUP_WIDTH), const),
            pl.BlockSpec((1, GROUP_WIDTH), const),
        ],
        out_specs=[
            pl.BlockSpec((tm, 2 * GROUP_WIDTH), lambda i: (i, 0)),
            pl.BlockSpec((tm, ZB_COLS), lambda i: (i, 0)),
            pl.BlockSpec((1, tm // blk, ZV_COLS, blk),
                         lambda i: (i // tiles_per_seq, i % tiles_per_seq, 0, 0)),
            pl.BlockSpec((tm, ZC_COLS), lambda i: (i, 0)),
            pl.BlockSpec((tm // MOBA_BLOCK, 1, GROUP_WIDTH), lambda i: (i, 0, 0)),
        ],
        out_shape=[
            jax.ShapeDtypeStruct((n, 2 * GROUP_WIDTH), jnp.bfloat16),
            jax.ShapeDtypeStruct((n, ZB_COLS), jnp.bfloat16),
            jax.ShapeDtypeStruct((batch, seq // blk, ZV_COLS, blk), jnp.bfloat16),
            jax.ShapeDtypeStruct((n, ZC_COLS), jnp.float32),
            jax.ShapeDtypeStruct((n // MOBA_BLOCK, 1, GROUP_WIDTH), jnp.float32),
        ],
        scratch_shapes=[pltpu.VMEM((HALO + tm, GROUP_WIDTH), jnp.float32),
                        pltpu.VMEM((HALO + tm, GROUP_WIDTH), jnp.float32)],
        compiler_params=pltpu.CompilerParams(
            dimension_semantics=("arbitrary",), vmem_limit_bytes=VMEM_LIMIT),
    )(x2d, x2d, g, w, wt, conv_w, pool_w_bd, pool_scale)


def _conv_pool(a_in, a_c, a_b, v, halo_h, halo_v, i, cw_ref, pw_ref, ps_ref, y_ref, hbuf, vbuf):
    ts = a_in.shape[0]
    gw = GROUP_WIDTH
    hbuf[0:HALO, :] = halo_h
    vbuf[0:HALO, :] = halo_v
    hbuf[HALO:HALO + ts, :] = a_c * a_in
    vbuf[HALO:HALO + ts, :] = v

    def hist(buf, d, lo, hi):
        return buf[HALO - d:HALO - d + ts, lo:hi]

    conv = (cw_ref[0:1, :] * hist(hbuf, 2, 0, gw) + cw_ref[1:2, :] * hist(hbuf, 1, 0, gw)
            + cw_ref[2:3, :] * hist(hbuf, 0, 0, gw))
    y_ref[:, 0:gw] = (a_b * conv).astype(y_ref.dtype)

    t_pos = i * ts + lax.broadcasted_iota(jnp.int32, (ts, LANES), 0)
    lane = lax.broadcasted_iota(jnp.int32, (ts, LANES), 1)
    first_group = lane < POOL_GROUP
    halves = []
    assert 2 * POOL_GROUP == LANES and POOL_WINDOWS[-1] <= HALO
    for half, (w_small, w_big) in enumerate((POOL_WINDOWS[0:2], POOL_WINDOWS[2:4])):
        lo, hi = half * LANES, (half + 1) * LANES
        acc = hist(vbuf, 0, lo, hi)
        for d in range(1, w_small):
            acc = acc + hist(vbuf, d, lo, hi)
        s_small = acc
        for d in range(w_small, w_big):
            acc = acc + hist(vbuf, d, lo, hi)
        wsum = jnp.where(first_group, s_small, acc)
        cnt = jnp.minimum(t_pos + 1, jnp.where(first_group, w_small, w_big)).astype(jnp.float32)
        halves.append(wsum / cnt - hist(vbuf, 0, lo, hi))
    dmat = jnp.concatenate(halves, axis=1).astype(jnp.bfloat16)
    yb = _dot(dmat, pw_ref[...]) * ps_ref[...]
    y_ref[:, gw:2 * gw] = yb.astype(y_ref.dtype)


def _attn_kernel(q_ref, k_ref, vt_ref, iq_ref, ik_ref, iw_ref, kmean_ref, tdiag_ref, tprev_ref,
                 o_ref, keys_scr, planes_scr, alive_scr, mb_scr, moba_scr, qpad_scr, m_scr,
                 acc_scr, *, topk):
    qi = pl.program_id(1)
    blk = ATT_BLOCK
    hd = HEAD_DIM
    key_idx = lax.broadcasted_iota(jnp.int32, (blk, blk), 0)
    qry_idx = lax.broadcasted_iota(jnp.int32, (blk, blk), 1)
    causal = key_idx <= qry_idx

    iq = iq_ref[...]
    idx_scale = (IDX_HEADS ** -0.5) * (IDX_DIM ** -0.5)
    iw_t = iw_ref[...].T * idx_scale

    lane_q = lax.broadcasted_iota(jnp.int32, (blk, LANES), 1)

    def score_keys(j):
        ik2 = ik_ref[pl.ds(pl.multiple_of(j * blk, blk), blk), :]
        ik_half = [jnp.where(lane_q < IDX_DIM, ik2, jnp.zeros_like(ik2)),
                   jnp.where(lane_q >= IDX_DIM, ik2, jnp.zeros_like(ik2))]
        acc = jnp.zeros((blk, blk), jnp.float32)
        for h in range(IDX_HEADS):
            s = _dot_t(ik_half[h % 2], iq[:, (h // 2) * LANES:(h // 2 + 1) * LANES])
            acc = acc + jnp.maximum(s, 0.0) * iw_t[h:h + 1, :]
        return acc + 0.0

    def to_key(score):
        bits = pltpu.bitcast(score, jnp.int32)
        return jnp.where(bits < 0, bits ^ jnp.int32(0x7FFFFFFF), bits)

    def store_keys(j, key):
        keys_scr[j] = key
        words = (key ^ jnp.int32(INT_MIN)).reshape(WORD_BITS, blk // WORD_BITS, blk)
        rows = [words[v] for v in range(WORD_BITS)]
        step, mask = WORD_BITS // 2, 0x0000FFFF
        while step:
            k = 0
            while k < WORD_BITS:
                t = (rows[k] ^ lax.shift_right_logical(rows[k + step], jnp.int32(step))) & mask
                rows[k] = rows[k] ^ t
                rows[k + step] = rows[k + step] ^ lax.shift_left(t, jnp.int32(step))
                k = (k + step + 1) & ~step
            step //= 2
            mask = (mask ^ (mask << step)) & 0xFFFFFFFF if step else mask
        for bit, plane in enumerate(rows):
            planes_scr[bit, j] = plane

    def score_group(i, carry):
        js = [SCORE_GROUP * i + c for c in range(SCORE_GROUP)]
        scores = [score_keys(j) for j in js]
        for j, sc in zip(js, scores):
            store_keys(j, to_key(sc))
        return carry

    lax.fori_loop(0, qi // SCORE_GROUP, score_group, 0)

    def prepare_attention():
        q8 = q_ref[...]
        nb = kmean_ref.shape[1]
        blk_idx = lax.broadcasted_iota(jnp.int32, (nb, blk), 0)
        blk_idx_f = blk_idx.astype(jnp.float32)
        km = kmean_ref[0]
        km_hi = km.astype(jnp.bfloat16)
        km_lo = (km - km_hi.astype(jnp.float32)).astype(jnp.bfloat16)
        for h in range(4):
            qh = q8[:, (4 + h) * hd:(5 + h) * hd]
            gate = _dot_t(km_hi[:, h * hd:(h + 1) * hd], qh) + _dot_t(km_lo[:, h * hd:(h + 1) * hd], qh)
            gate = jnp.where(blk_idx < qi, gate, -jnp.inf)
            chosen = jnp.zeros((nb, blk), jnp.bool_)
            for _ in range(MOBA_TOPB_MAX):
                mx = jnp.max(gate, axis=0, keepdims=True)
                is_mx = (gate == mx) & (gate > -jnp.inf)
                first = jnp.min(jnp.where(is_mx, blk_idx_f, float(nb)), axis=0, keepdims=True)
                pick = blk_idx_f == first
                chosen = chosen | pick
                gate = jnp.where(pick, -jnp.inf, gate)
            moba_scr[h] = jnp.where(chosen, 0.0, NEG)
        for h in range(N_HEADS):
            pair = q8[:, (h // 2) * LANES:(h // 2 + 1) * LANES]
            qpad_scr[h] = jnp.where((lane_q >= hd) == (h % 2 == 1), pair, jnp.zeros_like(pair))
        m_scr[...] = jnp.full(m_scr.shape, NEG, jnp.float32)
        acc_scr[...] = jnp.zeros(acc_scr.shape, jnp.float32)

    left_score = qi % SCORE_GROUP
    for r in range(SCORE_GROUP):
        @pl.when(left_score == r)
        def _(r=r):
            prepare_attention()
            js = [qi - r + c for c in range(r)]
            scores = [score_keys(j) for j in js]
            diag = score_keys(qi)
            for j, sc in zip(js, scores):
                store_keys(j, to_key(sc))
            store_keys(qi, jnp.where(causal, to_key(diag), jnp.int32(INT_MIN)))

    @pl.when((pl.program_id(0) == 0) & (qi == 0))
    def _():
        planes_scr[...] = jnp.zeros(planes_scr.shape, jnp.int32)

    chunk_idx = lax.broadcasted_iota(jnp.int32, alive_scr.shape, 0)
    alive_scr[...] = jnp.where(chunk_idx <= qi, jnp.int32(-1), jnp.int32(0))

    def bit_body(it, carry):
        thr_u, need = carry
        alive = alive_scr[...]
        with_bit = alive & planes_scr[it]
        ones = jnp.sum(jnp.sum(lax.population_count(with_bit), axis=0), axis=0, keepdims=True)
        take = ones >= need
        alive_scr[...] = jnp.where(take, with_bit, alive ^ with_bit)
        thr_u = jnp.where(take, thr_u | lax.shift_left(jnp.int32(1), 31 - it), thr_u)
        return thr_u, jnp.where(take, need, need - ones)

    thr_u, need = lax.fori_loop(0, WORD_BITS, bit_body,
                                (jnp.zeros((1, blk), jnp.int32), jnp.full((1, blk), topk, jnp.int32)))
    thr = thr_u ^ jnp.int32(INT_MIN)
    n_equal = jnp.sum(jnp.sum(lax.population_count(alive_scr[...]), axis=0), axis=0, keepdims=True)
    tie_any = jnp.max((n_equal - need).astype(jnp.float32)) > 0.0

    @pl.when(jnp.logical_not(tie_any))
    def _():
        def body(j, carry):
            mb_scr[j] = jnp.where(keys_scr[j] >= thr, 0.0, NEG)
            return carry
        lax.fori_loop(0, qi, body, 0)
        mb_scr[qi] = jnp.where((keys_scr[qi] >= thr) & causal, 0.0, NEG)

    @pl.when(tie_any)
    def _():
        places = need.astype(jnp.float32)
        lower =jnp.where(qry_idx <= key_idx, 1.0, 0.0).astype(jnp.bfloat16)

        def sel_chunk(j, base):
            kj = keys_scr[j]
            eq = kj == thr
            pref = _dot(lower, jnp.where(eq, 1.0, 0.0).astype(jnp.bfloat16)) + base
            return (kj > thr) | (eq & (pref <= places)), pref[blk - 1:blk, :]

        def body(j, base):
            sel, base = sel_chunk(j, base)
            mb_scr[j] = jnp.where(sel, 0.0, NEG)
            return base
        base = lax.fori_loop(0, qi, body, jnp.zeros((1, blk), jnp.float32))
        sel, _ = sel_chunk(qi, base)
        mb_scr[qi] = jnp.where(sel & causal, 0.0, NEG)

    ones_rows = jnp.ones((ACC_ROWS - hd, blk), jnp.bfloat16)

    def tile_rows(x, rows):
        return jnp.broadcast_to(x[None], (rows // SUBLANES, SUBLANES, blk)).reshape(rows, blk)

    def logits_phase(j, kind):
        kc = k_ref[pl.ds(pl.multiple_of(j * blk, blk), blk), :]
        staged = []
        for h in range(N_HEADS):
            s = _dot_t(kc[:, (h // 2) * LANES:(h // 2 + 1) * LANES], qpad_scr[h])
            if kind == "prev":
                s = s + tprev_ref[h]
            elif kind == "diag":
                s = s + tdiag_ref[h]
            if h < 4:
                s = s + mb_scr[j]
            elif kind == "diag":
                s = jnp.where(causal, s, NEG)
            mx = jnp.max(jnp.max(s.reshape(blk // SUBLANES, SUBLANES, blk), axis=0), axis=0, keepdims=True)
            m_prev = m_scr[h]
            m_new = jnp.maximum(m_prev, mx)
            shift = m_new
            if h >= 4 and kind != "diag":
                taken = moba_scr[h - 4, pl.ds(j, 1), :] == 0.0
                m_new = jnp.where(taken, m_new, m_prev)
                shift = jnp.where(taken, m_new, -NEG)
            m_scr[h] = m_new
            staged.append((s, shift, jnp.exp2(m_prev - m_new)))
        return staged

    def exp_phase(staged):
        return [(jnp.exp2(s - tile_rows(shift, blk)).astype(jnp.bfloat16), alpha)
                for s, shift, alpha in staged]

    def output_phase(j, probs):
        for h, (p, alpha) in enumerate(probs):
            vt_ones = jnp.concatenate([vt_ref[0, j, h * hd:(h + 1) * hd, :], ones_rows], axis=0)
            acc_scr[h] = tile_rows(alpha, ACC_ROWS) * acc_scr[h] + _dot(vt_ones, p)

    def attend(chunks):
        staged = [logits_phase(j, kind) for j, kind in chunks]
        probs = [exp_phase(st) for st in staged]
        for (j, _), pr in zip(chunks, probs):
            output_phase(j, pr)

    n_far = jnp.maximum(qi - 1, 0)

    def far_group(i, carry):
        attend([(FAR_GROUP * i + c, "far") for c in range(FAR_GROUP)])
        return carry

    lax.fori_loop(0, n_far // FAR_GROUP, far_group, 0)

    left = n_far % FAR_GROUP
    for r in range(FAR_GROUP):
        @pl.when((qi >= 1) & (left == r))
        def _(r=r):
            attend([(n_far - r + c, "far") for c in range(r)] + [(qi - 1, "prev"), (qi, "diag")])

    @pl.when(qi == 0)
    def _():
        attend([(qi, "diag")])

    out_t = jnp.concatenate(
        [acc_scr[h, 0:hd, :] / acc_scr[h, hd:hd + 1, :] for h in range(N_HEADS)], axis=0)
    o_ref[...] = out_t.T.astype(o_ref.dtype)


def _attention(zb, vt, zc, kmean, tdiag, tprev, batch, seq):
    blk = ATT_BLOCK
    nq = seq // blk
    nb = kmean.shape[1]
    topk = min(DSA_TOPK_MAX, seq // 4)
    kernel = functools.partial(_attn_kernel, topk=topk)
    resident = dict(pipeline_mode=pl.Buffered(1))
    return pl.pallas_call(
        kernel,
        grid=(batch, nq),
        in_specs=[
            pl.BlockSpec((blk, W8), lambda b, i: (b * nq + i, 0)),
            pl.BlockSpec((seq, W8), lambda b, i: (b, 1)),
            pl.BlockSpec((1, nq, W8, blk), lambda b, i: (b, 0, 0, 0)),
            pl.BlockSpec((blk, W8), lambda b, i: (b * nq + i, 2)),
            pl.BlockSpec((seq, LANES), lambda b, i: (b, 3 * W8 // LANES)),
            pl.BlockSpec((blk, LANES), lambda b, i: (b * nq + i, 0)),
            pl.BlockSpec((1, nb, GROUP_WIDTH), lambda b, i: (b, 0, 0)),
            pl.BlockSpec((N_HEADS, blk, blk), lambda b, i: (0, 0, 0), **resident),
            pl.BlockSpec((N_HEADS, blk, blk), lambda b, i: (0, 0, 0), **resident),
        ],
        out_specs=pl.BlockSpec((blk, W8), lambda b, i: (b * nq + i, 0)),
        out_shape=jax.ShapeDtypeStruct((batch * seq, W8), jnp.bfloat16),
        scratch_shapes=[
            pltpu.VMEM((nq, blk, blk), jnp.int32),
            pltpu.VMEM((WORD_BITS, nq, blk // WORD_BITS, blk), jnp.int32),
            pltpu.VMEM((nq, blk // WORD_BITS, blk), jnp.int32),
            pltpu.VMEM((nq, blk, blk), jnp.float32),
            pltpu.VMEM((4, nb, blk), jnp.float32),
            pltpu.VMEM((N_HEADS, blk, LANES), jnp.bfloat16),
            pltpu.VMEM((N_HEADS, SUBLANES, blk), jnp.float32),
            pltpu.VMEM((N_HEADS, ACC_ROWS, blk), jnp.float32),
        ],
        compiler_params=pltpu.CompilerParams(
            dimension_semantics=("arbitrary", "arbitrary"), vmem_limit_bytes=VMEM_LIMIT),
    )(zb, zb, vt, zb, zb, zc, kmean, tdiag, tprev)


def _tail_kernel(x_ref, yab_ref, ycd_ref, p_ref, wo_ref, gpost_ref, gfpre_ref, wgu_ref, wd_ref,
                 gfpost_ref, gple_ref, wpg_ref, wpp_ref, o_ref, act_scr):
    tm = x_ref.shape[0]
    half = yab_ref.shape[1]
    d_ff = wd_ref.shape[0]
    subs = [slice(r * TAIL_SUB, (r + 1) * TAIL_SUB) for r in range(tm // TAIL_SUB)]

    mix = [_dot(yab_ref[r, :], wo_ref[0:half, :]) + _dot(ycd_ref[r, :], wo_ref[half:2 * half, :])
           for r in subs]
    x1 = [x_ref[r, :] + _rms(m, gpost_ref[...]) for r, m in zip(subs, mix)]
    h2 = [_rms(x, gfpre_ref[...]).astype(jnp.bfloat16) for x in x1]
    for c0 in range(0, d_ff, FFN_CHUNK):
        cw = min(FFN_CHUNK, d_ff - c0)
        gates = [_dot(h, wgu_ref[:, c0:c0 + cw]) for h in h2]
        ups = [_dot(h, wgu_ref[:, d_ff + c0:d_ff + c0 + cw]) for h in h2]
        for r, g, u in zip(subs, gates, ups):
            act_scr[r, c0:c0 + cw] = (g * jax.nn.sigmoid(g) * u).astype(jnp.bfloat16)
    f = [_dot(act_scr[r, :], wd_ref[...]) for r in subs]
    x2 = [x + _rms(y, gfpost_ref[...]) for x, y in zip(x1, f)]
    hg = [_rms(x, gple_ref[...]).astype(jnp.bfloat16) for x in x2]
    gate = [jax.nn.sigmoid(_dot(h, wpg_ref[...])) for h in hg]
    proj = [_dot(p_ref[r, :].astype(jnp.bfloat16), wpp_ref[...]) for r in subs]
    for r, x, g, pr in zip(subs, x2, gate, proj):
        o_ref[r, :] = x + g * pr


def _layer_block(stacked, layer, **kwargs):
    zeros = (0,) * (stacked.ndim - 1)
    return pl.BlockSpec((None,) + stacked.shape[1:], lambda i: (layer,) + zeros, **kwargs)


def _tail(layer, x2d, yab, ycd, p3d, wo, gpost, gfpre, wgu, wd, gfpost, gple, wpg, wpp):
    n, d = x2d.shape
    tm = ROW_TILE
    row = lambda i: (i, 0)
    resident = dict(pipeline_mode=pl.Buffered(1))
    vec = pl.BlockSpec((1, d), lambda i: (0, 0))
    return pl.pallas_call(
        _tail_kernel,
        grid=(n // tm,),
        in_specs=[
            pl.BlockSpec((tm, d), row),
            pl.BlockSpec((tm, yab.shape[1]), row),
            pl.BlockSpec((tm, ycd.shape[1]), row),
            pl.BlockSpec((None, tm, p3d.shape[2]), lambda i: (layer, i, 0)),
            _layer_block(wo, layer, **resident),
            vec, vec,
            _layer_block(wgu, layer, **resident),
            _layer_block(wd, layer, **resident),
            vec, vec,
            _layer_block(wpg, layer, **resident),
            _layer_block(wpp, layer, **resident),
        ],
        out_specs=pl.BlockSpec((tm, d), row),
        out_shape=jax.ShapeDtypeStruct((n, d), jnp.float32),
        scratch_shapes=[pltpu.VMEM((tm, wd.shape[1]), jnp.bfloat16)],
        compiler_params=pltpu.CompilerParams(
            dimension_semantics=("arbitrary",), vmem_limit_bytes=VMEM_LIMIT),
    )(x2d, yab, ycd, p3d, wo, gpost, gfpre, wgu, wd, gfpost, gple, wpg, wpp)


def _rel_bucket_np(dist):
    n = np.maximum(dist, 0)
    max_exact = REL_BUCKETS // 2
    nf = np.maximum(n, 1).astype(np.float32)
    large = max_exact + (np.log(nf / np.float32(max_exact)) / np.float32(math.log(REL_MAX_DIST / max_exact))
                         * np.float32(REL_BUCKETS - max_exact)).astype(np.int32)
    large = np.minimum(large, REL_BUCKETS - 1)
    return np.where(n < max_exact, n, large)


def _bias_tables(rel_bias, seq):
    blk = ATT_BLOCK
    key = np.arange(blk)[:, None]
    qry = np.arange(blk)[None, :]
    far = _rel_bucket_np(np.arange(blk + 1, max(seq, blk + 2)))
    assert (far == far[0]).all(), "bias must be constant beyond the previous chunk"
    tab = rel_bias.astype(jnp.float32)
    tab = (tab - tab[:, int(far[0])][:, None]) * LOG2E

    def table(bucket):
        onehot = (jnp.asarray(bucket)[None] == jnp.arange(REL_BUCKETS)[:, None, None]).astype(jnp.float32)
        return jnp.einsum("hb,bkq->hkq", tab, onehot, precision=lax.Precision.HIGHEST)

    return table(_rel_bucket_np(qry - key)), table(_rel_bucket_np(qry - key + blk))


def _split_w_in(w_in):
    rest = w_in[..., MAIN_COLS:]
    sizes = (IDX_DIM, IDX_HEADS, GROUP_WIDTH, GROUP_WIDTH, GROUP_WIDTH)
    offs = np.concatenate([[0], np.cumsum(sizes)])
    ik, iw, dq, dk, dv = (rest[..., offs[i]:offs[i + 1]] for i in range(len(sizes)))
    pad = jnp.zeros(w_in.shape[:-1] + (LANES - IDX_HEADS,), w_in.dtype)
    tail = jnp.concatenate([dq, dk, dv, ik, ik, iw, pad], axis=-1)
    return w_in[..., :MAIN_COLS].astype(jnp.bfloat16), tail.astype(jnp.bfloat16)


def _block_diag(pool_w):
    depth, ng, g, _ = pool_w.shape
    eye = jnp.eye(ng, dtype=pool_w.dtype)
    bd = jnp.einsum("lgcd,gh->lgchd", pool_w, eye).reshape(depth, ng * g, ng * g)
    return bd.astype(jnp.bfloat16)


@jax.jit
def kernel(x, p, rel_bias, g_mix_pre, w_in, conv_w, pool_w, pool_scale, w_out, g_mix_post, g_ffn_pre, w_gate_up, w_down, g_ffn_post, g_ple, w_ple_gate, w_ple_proj):
    batch, seq, d = x.shape
    depth = w_in.shape[0]
    n = batch * seq
    d_ff = w_down.shape[1]
    assert seq % ROW_TILE == 0 and d_ff % LANES == 0
    bf16 = jnp.bfloat16

    tdiag, tprev = _bias_tables(rel_bias, seq)
    w_main, w_tail = _split_w_in(w_in)
    pool_bd = _block_diag(pool_w)
    wgu = w_gate_up.astype(bf16)
    wd = w_down.astype(bf16)
    wo = w_out.astype(bf16)
    wpg = w_ple_gate.astype(bf16)
    wpp = w_ple_proj.astype(bf16)

    x2d = x.reshape(n, d)
    p3d = p.reshape(depth, n, -1)
    for i in range(depth):
        yab, zb, vt, zc, kmean = _inproj(i, x2d, g_mix_pre[i][None, :], w_main, w_tail, conv_w[i],
                                         pool_bd[i], pool_scale[i][None, :], batch, seq)
        ycd = _attention(zb, vt, zc, kmean.reshape(batch, seq // MOBA_BLOCK, GROUP_WIDTH),
                         tdiag, tprev, batch, seq)
        x2d = _tail(i, x2d, yab, ycd, p3d, wo, g_mix_post[i][None, :], g_ffn_pre[i][None, :],
                    wgu, wd, g_ffn_post[i][None, :], g_ple[i][None, :], wpg, wpp)
    return x2d.reshape(batch, seq, d)
```

```python
import functools
import math

import numpy as np
import jax
import jax.numpy as jnp
from jax import lax
from jax.experimental import pallas as pl
from jax.experimental.pallas import tpu as pltpu

HEAD_DIM = 64
GROUP_WIDTH = 256
CONV_WIDTH = 3
POOL_WINDOWS = (2, 4, 8, 16)
POOL_GROUP = GROUP_WIDTH // len(POOL_WINDOWS)
IDX_HEADS = 8
IDX_DIM = 64
DSA_TOPK_MAX = 256
MOBA_BLOCK = 256
MOBA_TOPB_MAX = 3
REL_BUCKETS = 32
REL_MAX_DIST = 128
N_HEADS = 8
RMS_EPS = 1e-6

LANES = 128
SUBLANES = 8
ATT_BLOCK = 256
FAR_GROUP = 4
SCORE_GROUP = 4
HALO = 16
NEG = -1e30
LOG2E = math.log2(math.e)
ACC_ROWS = HEAD_DIM + 16
INT_MIN = -(2 ** 31)
WORD_BITS = 32
VMEM_LIMIT = 56 * 1024 * 1024
ROW_TILE = 512
TAIL_SUB = 256
FFN_CHUNK = 1024

gw_ = GROUP_WIDTH
SRC = dict(a_in=0, a_c=gw_, a_b=2 * gw_, pool_v=3 * gw_, cq=4 * gw_, ck=5 * gw_, cv=6 * gw_, iq=7 * gw_)
MAIN_COLS = 7 * gw_ + IDX_HEADS * IDX_DIM
TAIL = dict(dq=0, dk=gw_, dv=2 * gw_, ik2=3 * gw_, iw=3 * gw_ + LANES)
TAIL_COLS = 3 * gw_ + 2 * LANES
W8 = N_HEADS * HEAD_DIM
ZB_COLS = 2 * W8 + IDX_HEADS * IDX_DIM + 2 * IDX_DIM
ZV_COLS = W8
ZC_COLS = LANES
QSCALE = LOG2E * HEAD_DIM ** -0.5


def _rms(x, g):
    return x * lax.rsqrt(jnp.mean(x * x, axis=-1, keepdims=True) + RMS_EPS) * g


def _dot(a, b):
    return jnp.dot(a, b, preferred_element_type=jnp.float32)


def _dot_t(a, b):
    return lax.dot_general(a, b, (((1,), (1,)), ((), ())), preferred_element_type=jnp.float32)


def _inproj_kernel(x_ref, xprev_ref, g_ref, w_ref, wt_ref, cw_ref, pw_ref, ps_ref, yab_ref, zb_ref,
                   vt_ref, zc_ref, kmean_ref, hbuf, vbuf, *, tiles_per_seq):
    tm = x_ref.shape[0]
    blk = ATT_BLOCK
    gw = GROUP_WIDTH
    h = _rms(x_ref[...], g_ref[...]).astype(jnp.bfloat16)
    seq_tile = pl.program_id(0) % tiles_per_seq
    hp = _rms(xprev_ref[...], g_ref[...]).astype(jnp.bfloat16)
    h_ext = jnp.concatenate([hp, h], axis=0)

    def main_ext(name):
        z = _dot(h_ext, w_ref[:, SRC[name]:SRC[name] + gw])
        return z[0:HALO], z[HALO:]

    (p_in, a_in), (p_c, a_c), (p_v, pool_v) = main_ext("a_in"), main_ext("a_c"), main_ext("pool_v")
    halo_h = jnp.where(seq_tile == 0, 0.0, p_c * p_in)
    halo_v = jnp.where(seq_tile == 0, 0.0, p_v)

    def main(name, off=0, width=gw):
        c0 = SRC[name] + off
        return _dot(h, w_ref[:, c0:c0 + width])

    def tail(name, width=gw):
        return _dot(h, wt_ref[:, TAIL[name]:TAIL[name] + width])

    mixer_in = [a_in, a_c, main("a_b"), pool_v]
    zb_ref[:, 0:gw] = (main("cq") * QSCALE).astype(jnp.bfloat16)
    zb_ref[:, gw:2 * gw] = (tail("dq") * QSCALE).astype(jnp.bfloat16)
    zb_ref[:, 2 * gw:3 * gw] = main("ck").astype(jnp.bfloat16)
    dk = tail("dk")
    zb_ref[:, 3 * gw:4 * gw] = dk.astype(jnp.bfloat16)
    for r in range(tm // MOBA_BLOCK):
        kmean_ref[r] = jnp.mean(dk[r * MOBA_BLOCK:(r + 1) * MOBA_BLOCK], axis=0, keepdims=True)
    for c0 in range(0, IDX_HEADS * IDX_DIM, gw):
        zb_ref[:, 2 * W8 + c0:2 * W8 + c0 + gw] = main("iq", c0).astype(jnp.bfloat16)
    zb_ref[:, ZB_COLS - LANES:ZB_COLS] = tail("ik2", LANES).astype(jnp.bfloat16)
    for c0, z in ((0, main("cv")), (gw, tail("dv"))):
        for r in range(tm // blk):
            vt_ref[0, r, c0:c0 + gw, :] = z[r * blk:(r + 1) * blk, :].T.astype(jnp.bfloat16)
    zc_ref[...] = tail("iw", LANES)
    _conv_pool(*mixer_in, halo_h, halo_v, seq_tile, cw_ref, pw_ref, ps_ref, yab_ref, hbuf, vbuf)


def _inproj(layer, x2d, g, w, wt, conv_w, pool_w_bd, pool_scale, batch, seq):
    n, d = x2d.shape
    tm = ROW_TILE
    blk = ATT_BLOCK
    tiles_per_seq = seq // tm
    const = lambda i: (0, 0)
    return pl.pallas_call(
        functools.partial(_inproj_kernel, tiles_per_seq=tiles_per_seq),
        grid=(n // tm,),
        in_specs=[
            pl.BlockSpec((tm, d), lambda i: (i, 0)),
            pl.BlockSpec((HALO, d), lambda i: (jnp.maximum(i * (tm // HALO) - 1, 0), 0)),
            pl.BlockSpec((1, d), const),
            _layer_block(w, layer, pipeline_mode=pl.Buffered(1)),
            _layer_block(wt, layer, pipeline_mode=pl.Buffered(1)),
            pl.BlockSpec((CONV_WIDTH, GROUP_WIDTH), const),
            pl.BlockSpec((GROUP_WIDTH, GROUP_WIDTH), const),
            pl.BlockSpec((1, GROUP_WIDTH), const),
        ],
        out_specs=[
            pl.BlockSpec((tm, 2 * GROUP_WIDTH), lambda i: (i, 0)),
            pl.BlockSpec((tm, ZB_COLS), lambda i: (i, 0)),
            pl.BlockSpec((1, tm // blk, ZV_COLS, blk),
                         lambda i: (i // tiles_per_seq, i % tiles_per_seq, 0, 0)),
            pl.BlockSpec((tm, ZC_COLS), lambda i: (i, 0)),
            pl.BlockSpec((tm // MOBA_BLOCK, 1, GROUP_WIDTH), lambda i: (i, 0, 0)),
        ],
        out_shape=[
            jax.ShapeDtypeStruct((n, 2 * GROUP_WIDTH), jnp.bfloat16),
            jax.ShapeDtypeStruct((n, ZB_COLS), jnp.bfloat16),
            jax.ShapeDtypeStruct((batch, seq // blk, ZV_COLS, blk), jnp.bfloat16),
            jax.ShapeDtypeStruct((n, ZC_COLS), jnp.float32),
            jax.ShapeDtypeStruct((n // MOBA_BLOCK, 1, GROUP_WIDTH), jnp.float32),
        ],
        scratch_shapes=[pltpu.VMEM((HALO + tm, GROUP_WIDTH), jnp.float32),
                        pltpu.VMEM((HALO + tm, GROUP_WIDTH), jnp.float32)],
        compiler_params=pltpu.CompilerParams(
            dimension_semantics=("arbitrary",), vmem_limit_bytes=VMEM_LIMIT),
    )(x2d, x2d, g, w, wt, conv_w, pool_w_bd, pool_scale)


def _conv_pool(a_in, a_c, a_b, v, halo_h, halo_v, i, cw_ref, pw_ref, ps_ref, y_ref, hbuf, vbuf):
    ts = a_in.shape[0]
    gw = GROUP_WIDTH
    hbuf[0:HALO, :] = halo_h
    vbuf[0:HALO, :] = halo_v
    hbuf[HALO:HALO + ts, :] = a_c * a_in
    vbuf[HALO:HALO + ts, :] = v

    def hist(buf, d, lo, hi):
        return buf[HALO - d:HALO - d + ts, lo:hi]

    conv = (cw_ref[0:1, :] * hist(hbuf, 2, 0, gw) + cw_ref[1:2, :] * hist(hbuf, 1, 0, gw)
            + cw_ref[2:3, :] * hist(hbuf, 0, 0, gw))
    y_ref[:, 0:gw] = (a_b * conv).astype(y_ref.dtype)

    t_pos = i * ts + lax.broadcasted_iota(jnp.int32, (ts, LANES), 0)
    lane = lax.broadcasted_iota(jnp.int32, (ts, LANES), 1)
    first_group = lane < POOL_GROUP
    halves = []
    assert 2 * POOL_GROUP == LANES and POOL_WINDOWS[-1] <= HALO
    for half, (w_small, w_big) in enumerate((POOL_WINDOWS[0:2], POOL_WINDOWS[2:4])):
        lo, hi = half * LANES, (half + 1) * LANES
        acc = hist(vbuf, 0, lo, hi)
        for d in range(1, w_small):
            acc = acc + hist(vbuf, d, lo, hi)
        s_small = acc
        for d in range(w_small, w_big):
            acc = acc + hist(vbuf, d, lo, hi)
        wsum = jnp.where(first_group, s_small, acc)
        cnt = jnp.minimum(t_pos + 1, jnp.where(first_group, w_small, w_big)).astype(jnp.float32)
        halves.append(wsum / cnt - hist(vbuf, 0, lo, hi))
    dmat = jnp.concatenate(halves, axis=1).astype(jnp.bfloat16)
    yb = _dot(dmat, pw_ref[...]) * ps_ref[...]
    y_ref[:, gw:2 * gw] = yb.astype(y_ref.dtype)


def _attn_kernel(q_ref, k_ref, vt_ref, iq_ref, ik_ref, iw_ref, kmean_ref, tdiag_ref, tprev_ref,
                 o_ref, keys_scr, planes_scr, alive_scr, mb_scr, moba_scr, qpad_scr, m_scr,
                 acc_scr, *, topk):
    qi = pl.program_id(1)
    blk = ATT_BLOCK
    hd = HEAD_DIM
    key_idx = lax.broadcasted_iota(jnp.int32, (blk, blk), 0)
    qry_idx = lax.broadcasted_iota(jnp.int32, (blk, blk), 1)
    causal = key_idx <= qry_idx

    iq = iq_ref[...]
    idx_scale = (IDX_HEADS ** -0.5) * (IDX_DIM ** -0.5)
    iw_t = iw_ref[...].T * idx_scale

    lane_q = lax.broadcasted_iota(jnp.int32, (blk, LANES), 1)

    def score_keys(j):
        ik2 = ik_ref[pl.ds(pl.multiple_of(j * blk, blk), blk), :]
        ik_half = [jnp.where(lane_q < IDX_DIM, ik2, jnp.zeros_like(ik2)),
                   jnp.where(lane_q >= IDX_DIM, ik2, jnp.zeros_like(ik2))]
        acc = jnp.zeros((blk, blk), jnp.float32)
        for h in range(IDX_HEADS):
            s = _dot_t(ik_half[h % 2], iq[:, (h // 2) * LANES:(h // 2 + 1) * LANES])
            acc = acc + jnp.maximum(s, 0.0) * iw_t[h:h + 1, :]
        return acc + 0.0

    def to_key(score):
        bits = pltpu.bitcast(score, jnp.int32)
        return jnp.where(bits < 0, bits ^ jnp.int32(0x7FFFFFFF), bits)

    def store_keys(j, key):
        keys_scr[j] = key
        words = (key ^ jnp.int32(INT_MIN)).reshape(WORD_BITS, blk // WORD_BITS, blk)
        rows = [words[v] for v in range(WORD_BITS)]
        step, mask = WORD_BITS // 2, 0x0000FFFF
        while step:
            k = 0
            while k < WORD_BITS:
                t = (rows[k] ^ lax.shift_right_logical(rows[k + step], jnp.int32(step))) & mask
                rows[k] = rows[k] ^ t
                rows[k + step] = rows[k + step] ^ lax.shift_left(t, jnp.int32(step))
                k = (k + step + 1) & ~step
            step //= 2
            mask = (mask ^ (mask << step)) & 0xFFFFFFFF if step else mask
        for bit, plane in enumerate(rows):
            planes_scr[bit, j] = plane

    def score_group(i, carry):
        js = [SCORE_GROUP * i + c for c in range(SCORE_GROUP)]
        scores = [score_keys(j) for j in js]
        for j, sc in zip(js, scores):
            store_keys(j, to_key(sc))
        return carry

    lax.fori_loop(0, qi // SCORE_GROUP, score_group, 0)

    def prepare_attention():
        q8 = q_ref[...]
        nb = kmean_ref.shape[1]
        blk_idx = lax.broadcasted_iota(jnp.int32, (nb, blk), 0)
        blk_idx_f = blk_idx.astype(jnp.float32)
        km = kmean_ref[0]
        km_hi = km.astype(jnp.bfloat16)
        km_lo = (km - km_hi.astype(jnp.float32)).astype(jnp.bfloat16)
        for h in range(4):
            qh = q8[:, (4 + h) * hd:(5 + h) * hd]
            gate = _dot_t(km_hi[:, h * hd:(h + 1) * hd], qh) + _dot_t(km_lo[:, h * hd:(h + 1) * hd], qh)
            gate = jnp.where(blk_idx < qi, gate, -jnp.inf)
            chosen = jnp.zeros((nb, blk), jnp.bool_)
            for _ in range(MOBA_TOPB_MAX):
                mx = jnp.max(gate, axis=0, keepdims=True)
                is_mx = (gate == mx) & (gate > -jnp.inf)
                first = jnp.min(jnp.where(is_mx, blk_idx_f, float(nb)), axis=0, keepdims=True)
                pick = blk_idx_f == first
                chosen = chosen | pick
                gate = jnp.where(pick, -jnp.inf, gate)
            moba_scr[h] = jnp.where(chosen, 0.0, NEG)
        for h in range(N_HEADS):
            pair = q8[:, (h // 2) * LANES:(h // 2 + 1) * LANES]
            qpad_scr[h] = jnp.where((lane_q >= hd) == (h % 2 == 1), pair, jnp.zeros_like(pair))
        m_scr[...] = jnp.full(m_scr.shape, NEG, jnp.float32)
        acc_scr[...] = jnp.zeros(acc_scr.shape, jnp.float32)

    left_score = qi % SCORE_GROUP
    for r in range(SCORE_GROUP):
        @pl.when(left_score == r)
        def _(r=r):
            prepare_attention()
            js = [qi - r + c for c in range(r)]
            scores = [score_keys(j) for j in js]
            diag = score_keys(qi)
            for j, sc in zip(js, scores):
                store_keys(j, to_key(sc))
            store_keys(qi, jnp.where(causal, to_key(diag), jnp.int32(INT_MIN)))

    @pl.when((pl.program_id(0) == 0) & (qi == 0))
    def _():
        planes_scr[...] = jnp.zeros(planes_scr.shape, jnp.int32)

    chunk_idx = lax.broadcasted_iota(jnp.int32, alive_scr.shape, 0)
    alive_scr[...] = jnp.where(chunk_idx <= qi, jnp.int32(-1), jnp.int32(0))

    def bit_body(it, carry):
        thr_u, need = carry
        alive = alive_scr[...]
        with_bit = alive & planes_scr[it]
        ones = jnp.sum(jnp.sum(lax.population_count(with_bit), axis=0), axis=0, keepdims=True)
        take = ones >= need
        alive_scr[...] = jnp.where(take, with_bit, alive ^ with_bit)
        thr_u = jnp.where(take, thr_u | lax.shift_left(jnp.int32(1), 31 - it), thr_u)
        return thr_u, jnp.where(take, need, need - ones)

    carry = (jnp.zeros((1, blk), jnp.int32), jnp.full((1, blk), topk, jnp.int32))
    for it in range(WORD_BITS):
        carry = bit_body(it, carry)
    thr_u, need = carry
    thr = thr_u ^ jnp.int32(INT_MIN)
    n_equal = jnp.sum(jnp.sum(lax.population_count(alive_scr[...]), axis=0), axis=0, keepdims=True)
    tie_any = jnp.max((n_equal - need).astype(jnp.float32)) > 0.0

    @pl.when(jnp.logical_not(tie_any))
    def _():
        def body(j, carry):
            mb_scr[j] = jnp.where(keys_scr[j] >= thr, 0.0, NEG)
            return carry
        lax.fori_loop(0, qi, body, 0)
        mb_scr[qi] = jnp.where((keys_scr[qi] >= thr) & causal, 0.0, NEG)

    @pl.when(tie_any)
    def _():
        places = need.astype(jnp.float32)
        lower =jnp.where(qry_idx <= key_idx, 1.0, 0.0).astype(jnp.bfloat16)

        def sel_chunk(j, base):
            kj = keys_scr[j]
            eq = kj == thr
            pref = _dot(lower, jnp.where(eq, 1.0, 0.0).astype(jnp.bfloat16)) + base
            return (kj > thr) | (eq & (pref <= places)), pref[blk - 1:blk, :]

        def body(j, base):
            sel, base = sel_chunk(j, base)
            mb_scr[j] = jnp.where(sel, 0.0, NEG)
            return base
        base = lax.fori_loop(0, qi, body, jnp.zeros((1, blk), jnp.float32))
        sel, _ = sel_chunk(qi, base)
        mb_scr[qi] = jnp.where(sel & causal, 0.0, NEG)

    ones_rows = jnp.ones((ACC_ROWS - hd, blk), jnp.bfloat16)

    def tile_rows(x, rows):
        return jnp.broadcast_to(x[None], (rows // SUBLANES, SUBLANES, blk)).reshape(rows, blk)

    def logits_phase(j, kind):
        kc = k_ref[pl.ds(pl.multiple_of(j * blk, blk), blk), :]
        staged = []
        for h in range(N_HEADS):
            s = _dot_t(kc[:, (h // 2) * LANES:(h // 2 + 1) * LANES], qpad_scr[h])
            if kind == "prev":
                s = s + tprev_ref[h]
            elif kind == "diag":
                s = s + tdiag_ref[h]
            if h < 4:
                s = s + mb_scr[j]
            elif kind == "diag":
                s = jnp.where(causal, s, NEG)
            mx = jnp.max(jnp.max(s.reshape(blk // SUBLANES, SUBLANES, blk), axis=0), axis=0, keepdims=True)
            m_prev = m_scr[h]
            m_new = jnp.maximum(m_prev, mx)
            shift = m_new
            if h >= 4 and kind != "diag":
                taken = moba_scr[h - 4, pl.ds(j, 1), :] == 0.0
                m_new = jnp.where(taken, m_new, m_prev)
                shift = jnp.where(taken, m_new, -NEG)
            m_scr[h] = m_new
            staged.append((s, shift, jnp.exp2(m_prev - m_new)))
        return staged

    def exp_phase(staged):
        return [(jnp.exp2(s - tile_rows(shift, blk)).astype(jnp.bfloat16), alpha)
                for s, shift, alpha in staged]

    def output_phase(j, probs):
        for h, (p, alpha) in enumerate(probs):
            vt_ones = jnp.concatenate([vt_ref[0, j, h * hd:(h + 1) * hd, :], ones_rows], axis=0)
            acc_scr[h] = tile_rows(alpha, ACC_ROWS) * acc_scr[h] + _dot(vt_ones, p)

    def attend(chunks):
        staged = [logits_phase(j, kind) for j, kind in chunks]
        probs = [exp_phase(st) for st in staged]
        for (j, _), pr in zip(chunks, probs):
            output_phase(j, pr)

    n_far = jnp.maximum(qi - 1, 0)

    def far_group(i, carry):
        attend([(FAR_GROUP * i + c, "far") for c in range(FAR_GROUP)])
        return carry

    lax.fori_loop(0, n_far // FAR_GROUP, far_group, 0)

    left = n_far % FAR_GROUP
    for r in range(FAR_GROUP):
        @pl.when((qi >= 1) & (left == r))
        def _(r=r):
            attend([(n_far - r + c, "far") for c in range(r)] + [(qi - 1, "prev"), (qi, "diag")])

    @pl.when(qi == 0)
    def _():
        attend([(qi, "diag")])

    out_t = jnp.concatenate(
        [acc_scr[h, 0:hd, :] / acc_scr[h, hd:hd + 1, :] for h in range(N_HEADS)], axis=0)
    o_ref[...] = out_t.T.astype(o_ref.dtype)


def _attention(zb, vt, zc, kmean, tdiag, tprev, batch, seq):
    blk = ATT_BLOCK
    nq = seq // blk
    nb = kmean.shape[1]
    topk = min(DSA_TOPK_MAX, seq // 4)
    kernel = functools.partial(_attn_kernel, topk=topk)
    resident = dict(pipeline_mode=pl.Buffered(1))
    return pl.pallas_call(
        kernel,
        grid=(batch, nq),
        in_specs=[
            pl.BlockSpec((blk, W8), lambda b, i: (b * nq + i, 0)),
            pl.BlockSpec((seq, W8), lambda b, i: (b, 1)),
            pl.BlockSpec((1, nq, W8, blk), lambda b, i: (b, 0, 0, 0)),
            pl.BlockSpec((blk, W8), lambda b, i: (b * nq + i, 2)),
            pl.BlockSpec((seq, LANES), lambda b, i: (b, 3 * W8 // LANES)),
            pl.BlockSpec((blk, LANES), lambda b, i: (b * nq + i, 0)),
            pl.BlockSpec((1, nb, GROUP_WIDTH), lambda b, i: (b, 0, 0)),
            pl.BlockSpec((N_HEADS, blk, blk), lambda b, i: (0, 0, 0), **resident),
            pl.BlockSpec((N_HEADS, blk, blk), lambda b, i: (0, 0, 0), **resident),
        ],
        out_specs=pl.BlockSpec((blk, W8), lambda b, i: (b * nq + i, 0)),
        out_shape=jax.ShapeDtypeStruct((batch * seq, W8), jnp.bfloat16),
        scratch_shapes=[
            pltpu.VMEM((nq, blk, blk), jnp.int32),
            pltpu.VMEM((WORD_BITS, nq, blk // WORD_BITS, blk), jnp.int32),
            pltpu.VMEM((nq, blk // WORD_BITS, blk), jnp.int32),
            pltpu.VMEM((nq, blk, blk), jnp.float32),
            pltpu.VMEM((4, nb, blk), jnp.float32),
            pltpu.VMEM((N_HEADS, blk, LANES), jnp.bfloat16),
            pltpu.VMEM((N_HEADS, SUBLANES, blk), jnp.float32),
            pltpu.VMEM((N_HEADS, ACC_ROWS, blk), jnp.float32),
        ],
        compiler_params=pltpu.CompilerParams(
            dimension_semantics=("arbitrary", "arbitrary"), vmem_limit_bytes=VMEM_LIMIT),
    )(zb, zb, vt, zb, zb, zc, kmean, tdiag, tprev)


def _tail_kernel(x_ref, yab_ref, ycd_ref, p_ref, wo_ref, gpost_ref, gfpre_ref, wgu_ref, wd_ref,
                 gfpost_ref, gple_ref, wpg_ref, wpp_ref, o_ref, act_scr):
    tm = x_ref.shape[0]
    half = yab_ref.shape[1]
    d_ff = wd_ref.shape[0]
    subs = [slice(r * TAIL_SUB, (r + 1) * TAIL_SUB) for r in range(tm // TAIL_SUB)]

    mix = [_dot(yab_ref[r, :], wo_ref[0:half, :]) + _dot(ycd_ref[r, :], wo_ref[half:2 * half, :])
           for r in subs]
    x1 = [x_ref[r, :] + _rms(m, gpost_ref[...]) for r, m in zip(subs, mix)]
    h2 = [_rms(x, gfpre_ref[...]).astype(jnp.bfloat16) for x in x1]
    for c0 in range(0, d_ff, FFN_CHUNK):
        cw = min(FFN_CHUNK, d_ff - c0)
        gates = [_dot(h, wgu_ref[:, c0:c0 + cw]) for h in h2]
        ups = [_dot(h, wgu_ref[:, d_ff + c0:d_ff + c0 + cw]) for h in h2]
        for r, g, u in zip(subs, gates, ups):
            act_scr[r, c0:c0 + cw] = (g * jax.nn.sigmoid(g) * u).astype(jnp.bfloat16)
    f = [_dot(act_scr[r, :], wd_ref[...]) for r in subs]
    x2 = [x + _rms(y, gfpost_ref[...]) for x, y in zip(x1, f)]
    hg = [_rms(x, gple_ref[...]).astype(jnp.bfloat16) for x in x2]
    gate = [jax.nn.sigmoid(_dot(h, wpg_ref[...])) for h in hg]
    proj = [_dot(p_ref[r, :].astype(jnp.bfloat16), wpp_ref[...]) for r in subs]
    for r, x, g, pr in zip(subs, x2, gate, proj):
        o_ref[r, :] = x + g * pr


def _layer_block(stacked, layer, **kwargs):
    zeros = (0,) * (stacked.ndim - 1)
    return pl.BlockSpec((None,) + stacked.shape[1:], lambda i: (layer,) + zeros, **kwargs)


def _tail(layer, x2d, yab, ycd, p3d, wo, gpost, gfpre, wgu, wd, gfpost, gple, wpg, wpp):
    n, d = x2d.shape
    tm = ROW_TILE
    row = lambda i: (i, 0)
    resident = dict(pipeline_mode=pl.Buffered(1))
    vec = pl.BlockSpec((1, d), lambda i: (0, 0))
    return pl.pallas_call(
        _tail_kernel,
        grid=(n // tm,),
        in_specs=[
            pl.BlockSpec((tm, d), row),
            pl.BlockSpec((tm, yab.shape[1]), row),
            pl.BlockSpec((tm, ycd.shape[1]), row),
            pl.BlockSpec((None, tm, p3d.shape[2]), lambda i: (layer, i, 0)),
            _layer_block(wo, layer, **resident),
            vec, vec,
            _layer_block(wgu, layer, **resident),
            _layer_block(wd, layer, **resident),
            vec, vec,
            _layer_block(wpg, layer, **resident),
            _layer_block(wpp, layer, **resident),
        ],
        out_specs=pl.BlockSpec((tm, d), row),
        out_shape=jax.ShapeDtypeStruct((n, d), jnp.float32),
        scratch_shapes=[pltpu.VMEM((tm, wd.shape[1]), jnp.bfloat16)],
        compiler_params=pltpu.CompilerParams(
            dimension_semantics=("arbitrary",), vmem_limit_bytes=VMEM_LIMIT),
    )(x2d, yab, ycd, p3d, wo, gpost, gfpre, wgu, wd, gfpost, gple, wpg, wpp)


def _rel_bucket_np(dist):
    n = np.maximum(dist, 0)
    max_exact = REL_BUCKETS // 2
    nf = np.maximum(n, 1).astype(np.float32)
    large = max_exact + (np.log(nf / np.float32(max_exact)) / np.float32(math.log(REL_MAX_DIST / max_exact))
                         * np.float32(REL_BUCKETS - max_exact)).astype(np.int32)
    large = np.minimum(large, REL_BUCKETS - 1)
    return np.where(n < max_exact, n, large)


def _bias_tables(rel_bias, seq):
    blk = ATT_BLOCK
    key = np.arange(blk)[:, None]
    qry = np.arange(blk)[None, :]
    far = _rel_bucket_np(np.arange(blk + 1, max(seq, blk + 2)))
    assert (far == far[0]).all(), "bias must be constant beyond the previous chunk"
    tab = rel_bias.astype(jnp.float32)
    tab = (tab - tab[:, int(far[0])][:, None]) * LOG2E

    def table(bucket):
        onehot = (jnp.asarray(bucket)[None] == jnp.arange(REL_BUCKETS)[:, None, None]).astype(jnp.float32)
        return jnp.einsum("hb,bkq->hkq", tab, onehot, precision=lax.Precision.HIGHEST)

    return table(_rel_bucket_np(qry - key)), table(_rel_bucket_np(qry - key + blk))


def _split_w_in(w_in):
    rest = w_in[..., MAIN_COLS:]
    sizes = (IDX_DIM, IDX_HEADS, GROUP_WIDTH, GROUP_WIDTH, GROUP_WIDTH)
    offs = np.concatenate([[0], np.cumsum(sizes)])
    ik, iw, dq, dk, dv = (rest[..., offs[i]:offs[i + 1]] for i in range(len(sizes)))
    pad = jnp.zeros(w_in.shape[:-1] + (LANES - IDX_HEADS,), w_in.dtype)
    tail = jnp.concatenate([dq, dk, dv, ik, ik, iw, pad], axis=-1)
    return w_in[..., :MAIN_COLS].astype(jnp.bfloat16), tail.astype(jnp.bfloat16)


def _block_diag(pool_w):
    depth, ng, g, _ = pool_w.shape
    eye = jnp.eye(ng, dtype=pool_w.dtype)
    bd = jnp.einsum("lgcd,gh->lgchd", pool_w, eye).reshape(depth, ng * g, ng * g)
    return bd.astype(jnp.bfloat16)


@jax.jit
def kernel(x, p, rel_bias, g_mix_pre, w_in, conv_w, pool_w, pool_scale, w_out, g_mix_post, g_ffn_pre, w_gate_up, w_down, g_ffn_post, g_ple, w_ple_gate, w_ple_proj):
    batch, seq, d = x.shape
    depth = w_in.shape[0]
    n = batch * seq
    d_ff = w_down.shape[1]
    assert seq % ROW_TILE == 0 and d_ff % LANES == 0
    bf16 = jnp.bfloat16

    tdiag, tprev = _bias_tables(rel_bias, seq)
    w_main, w_tail = _split_w_in(w_in)
    pool_bd = _block_diag(pool_w)
    wgu = w_gate_up.astype(bf16)
    wd = w_down.astype(bf16)
    wo = w_out.astype(bf16)
    wpg = w_ple_gate.astype(bf16)
    wpp = w_ple_proj.astype(bf16)

    x2d = x.reshape(n, d)
    p3d = p.reshape(depth, n, -1)
    for i in range(depth):
        yab, zb, vt, zc, kmean = _inproj(i, x2d, g_mix_pre[i][None, :], w_main, w_tail, conv_w[i],
                                         pool_bd[i], pool_scale[i][None, :], batch, seq)
        ycd = _attention(zb, vt, zc, kmean.reshape(batch, seq // MOBA_BLOCK, GROUP_WIDTH),
                         tdiag, tprev, batch, seq)
        x2d = _tail(i, x2d, yab, ycd, p3d, wo, g_mix_post[i][None, :], g_ffn_pre[i][None, :],
                    wgu, wd, g_ffn_post[i][None, :], g_ple[i][None, :], wpg, wpp)
    return x2d.reshape(batch, seq, d)
```
